```python
import jax, jax.numpy as jnp
from jax import lax
import numpy as np

D_MODEL = 1024
BATCH = 4
SEQ = 4096
DEPTH = 2

CHUNK = 64
N_META = 16
Q_BLOCK = 128
A_HEADS = 8
A_KV_HEADS = 2
A_HEAD_DIM = 64
IDX_HEADS = 8
IDX_DIM = 32
TOPK_MAX = 256
A_Q = A_HEADS * A_HEAD_DIM
A_KV = A_KV_HEADS * A_HEAD_DIM
IDX_Q = IDX_HEADS * IDX_DIM
A_COLS = A_Q + 2 * A_KV + IDX_Q + IDX_DIM + IDX_HEADS
R_HEADS = 8
R_HEAD_DIM = 64
R_WIDTH = R_HEADS * R_HEAD_DIM
W_LORA = 64
A_LORA = 64
G_LORA = 128
R_COLS = 3 * R_WIDTH + W_LORA + A_LORA + G_LORA
GN_EPS = 64e-5
GATE_COLS = 2 * D_MODEL
IN_COLS = A_COLS + R_COLS + GATE_COLS
N_EXPERTS = 16
N_GROUPS = 4
EXPERTS_PER_GROUP = N_EXPERTS // N_GROUPS
TOP_K_EXPERTS = 2
D_EXPERT = 512
LN_EPS = 1e-5
ALPHA = (2 * DEPTH) ** 0.25
BETA = (8 * DEPTH) ** -0.25

kernel_name = 'hybrid_dsa_rwkv7_grouped_moe_deepnorm'


def layer_norm(x, g, b):
    xf = x.astype(jnp.float32)
    mu = jnp.mean(xf, axis=-1, keepdims=True)
    var = jnp.mean(jnp.square(xf - mu), axis=-1, keepdims=True)
    return ((xf - mu) * lax.rsqrt(var + LN_EPS) * g.astype(jnp.float32) + b.astype(jnp.float32)).astype(x.dtype)


def split_cols(u, sizes):
    outs, off = [], 0
    for s in sizes:
        outs.append(u[..., off:off + s])
        off += s
    return outs


def chunk_ids(n):
    p = jnp.arange(n)
    return jnp.where(p < N_META, 0, 1 + (p - N_META) // CHUNK)


def dsa_attention(q, k, v, qi, ki, wi, topk):
    B, L = q.shape[0], q.shape[1]
    n_blocks = -(-L // Q_BLOCK)
    Lp = n_blocks * Q_BLOCK
    pad = Lp - L
    padq = lambda a: jnp.pad(a, [(0, 0), (0, pad)] + [(0, 0)] * (a.ndim - 2))
    q, qi, wi = padq(q), padq(qi), padq(wi)
    cid_k = chunk_ids(L)
    cid_q = chunk_ids(Lp)
    scale = A_HEAD_DIM ** -0.5
    group = A_HEADS // A_KV_HEADS
    gather = jax.vmap(lambda a, ii: a[ii])

    def block(i):
        s0 = i * Q_BLOCK
        qb = lax.dynamic_slice_in_dim(q, s0, Q_BLOCK, axis=1)
        qib = lax.dynamic_slice_in_dim(qi, s0, Q_BLOCK, axis=1)
        wib = lax.dynamic_slice_in_dim(wi, s0, Q_BLOCK, axis=1)
        cq = lax.dynamic_slice_in_dim(cid_q, s0, Q_BLOCK)
        rel = jax.nn.relu(jnp.einsum('bqhd,bsd->bqhs', qib, ki)).astype(jnp.float32)
        idx_score = jnp.einsum('bqh,bqhs->bqs', wib.astype(jnp.float32), rel)
        admissible = cid_k[None, :] <= cq[:, None]
        idx_score = jnp.where(admissible[None], idx_score, -jnp.inf)
        _, sel = lax.top_k(idx_score, topk)
        ks = gather(k, sel)
        vs = gather(v, sel)
        valid = cid_k[sel] <= cq[None, :, None]
        qg = qb.reshape(B, Q_BLOCK, A_KV_HEADS, group, A_HEAD_DIM)
        s = jnp.einsum('bqngd,bqknd->bqngk', qg, ks).astype(jnp.float32) * scale
        s = jnp.where(valid[:, :, None, None, :], s, -jnp.inf)
        p = jax.nn.softmax(s, axis=-1).astype(vs.dtype)
        o = jnp.einsum('bqngk,bqknd->bqngd', p, vs)
        return o.reshape(B, Q_BLOCK, A_Q)

    out = lax.map(block, jnp.arange(n_blocks))
    out = jnp.moveaxis(out, 0, 1).reshape(B, Lp, A_Q)
    return out[:, :L]


def rwkv7_time_mix(u, mu, w0, w_up, a0, a_up, g_up, k_k, k_a, r_k, gn_g, gn_b):
    B, L, _ = u.shape
    dt = u.dtype
    u = u.astype(jnp.float32)
    u_prev = jnp.pad(u, ((0, 0), (1, 0), (0, 0)))[:, :-1]
    u = u + (u_prev - u) * mu.astype(jnp.float32)
    r, k, v, xw, xa, xg = split_cols(u, [R_WIDTH, R_WIDTH, R_WIDTH, W_LORA, A_LORA, G_LORA])
    f32 = lambda t: t.astype(jnp.float32)
    logw = -jax.nn.softplus(-(f32(w0) + jnp.tanh(xw) @ f32(w_up))) - 0.5
    decay = jnp.exp(-jnp.exp(logw))
    a = jax.nn.sigmoid(f32(a0) + xa @ f32(a_up))
    g = jax.nn.sigmoid(xg) @ f32(g_up)
    hd = lambda t: t.reshape(B, L, R_HEADS, R_HEAD_DIM)
    kk = hd(k * f32(k_k))
    kk = kk * lax.rsqrt(jnp.maximum(jnp.sum(kk * kk, axis=-1, keepdims=True), 1e-24))
    k = k * (1.0 + (a - 1.0) * f32(k_a))
    rh, kh, vh, wh, ah = hd(r), hd(k), hd(v), hd(decay), hd(a)

    def step(S, inp):
        r_t, w_t, kk_t, a_t, v_t, k_t = inp
        sa = jnp.einsum('bhvk,bhk->bhv', S, -kk_t)
        S = S * w_t[:, :, None, :] + sa[..., None] * (kk_t * a_t)[:, :, None, :] + v_t[..., None] * k_t[:, :, None, :]
        y = jnp.einsum('bhvk,bhk->bhv', S, r_t)
        return S, y

    xs = tuple(jnp.moveaxis(t, 1, 0) for t in (rh, wh, kk, ah, vh, kh))
    S0 = jnp.zeros((B, R_HEADS, R_HEAD_DIM, R_HEAD_DIM), jnp.float32)
    _, y = lax.scan(step, S0, xs)
    y = jnp.moveaxis(y, 0, 1)
    m = jnp.mean(y, axis=-1, keepdims=True)
    var = jnp.mean(jnp.square(y - m), axis=-1, keepdims=True)
    y = ((y - m) * lax.rsqrt(var + GN_EPS)).reshape(B, L, R_WIDTH) * f32(gn_g) + f32(gn_b)
    bonus = (jnp.sum(rh * kh * f32(r_k), axis=-1, keepdims=True) * vh).reshape(B, L, R_WIDTH)
    return ((y + bonus) * g).astype(dt)


def token_mixers(h, w_in, b_gate, mu, w0, w_up, a0, a_up, g_up, k_k, k_a, r_k, gn_g, gn_b, w_branch_a, w_branch_b, w_out, topk):
    B, L, _ = h.shape
    u = h @ w_in
    ua, ur, ug = split_cols(u, [A_COLS, R_COLS, GATE_COLS])
    q, k, v, qi, ki, wi = split_cols(ua, [A_Q, A_KV, A_KV, IDX_Q, IDX_DIM, IDX_HEADS])
    o_a = dsa_attention(q.reshape(B, L, A_HEADS, A_HEAD_DIM), k.reshape(B, L, A_KV_HEADS, A_HEAD_DIM),
                        v.reshape(B, L, A_KV_HEADS, A_HEAD_DIM), qi.reshape(B, L, IDX_HEADS, IDX_DIM), ki, wi, topk)
    o_b = rwkv7_time_mix(ur, mu, w0, w_up, a0, a_up, g_up, k_k, k_a, r_k, gn_g, gn_b)
    gates = jax.nn.sigmoid((ug + b_gate).astype(jnp.float32)).astype(h.dtype)
    g_a, g_b = split_cols(gates, [D_MODEL, D_MODEL])
    merged = g_a * (o_a @ w_branch_a) + g_b * (o_b @ w_branch_b)
    return merged @ w_out


def grouped_moe(h, w_router, b_router, w1, w3, w2):
    B, L, D = h.shape
    x = h.reshape(B * L, D)
    T = x.shape[0]
    s = jax.nn.sigmoid((x @ w_router).astype(jnp.float32))
    sel = s + b_router.astype(jnp.float32)
    sg = sel.reshape(T, N_GROUPS, EXPERTS_PER_GROUP)
    group_score = jnp.sum(lax.top_k(sg, TOP_K_EXPERTS)[0], axis=-1)
    g_star = jnp.argmax(group_score, axis=-1)
    in_group = sg[jnp.arange(T), g_star]
    _, j = lax.top_k(in_group, TOP_K_EXPERTS)
    experts = g_star[:, None] * EXPERTS_PER_GROUP + j
    gate = jnp.take_along_axis(s, experts, axis=-1)
    gate = gate / jnp.sum(gate, axis=-1, keepdims=True)
    combine = jnp.sum(jax.nn.one_hot(experts, N_EXPERTS, dtype=jnp.float32) * gate[..., None], axis=1).astype(x.dtype)
    y = jnp.zeros_like(x)
    for e in range(N_EXPERTS):
        he = (jax.nn.silu(x @ w1[e]) * (x @ w3[e])) @ w2[e]
        y = y + combine[:, e:e + 1] * he
    return y.reshape(B, L, D)


def setup_inputs(seed: int = 0) -> dict:
    key = jax.random.key(seed)
    ks = jax.random.split(key, 32)
    nrm = lambda k, shape, s: jax.random.normal(k, shape, jnp.float32) * s
    uni = lambda k, shape, lo, hi: jax.random.uniform(k, shape, jnp.float32, lo, hi)
    return {
        'x': nrm(ks[0], (BATCH, SEQ, D_MODEL), 1.0),
        'meta_tokens': nrm(ks[1], (N_META, D_MODEL), 1.0),
        'ln_in_g': 1.0 + nrm(ks[2], (D_MODEL,), 0.02),
        'ln_in_b': nrm(ks[3], (D_MODEL,), 0.02),
        'w_in': nrm(ks[4], (DEPTH, D_MODEL, IN_COLS), D_MODEL ** -0.5),
        'b_gate': nrm(ks[5], (DEPTH, GATE_COLS), 0.02),
        'rwkv_mu': uni(ks[6], (DEPTH, R_COLS), 0.0, 1.0),
        'rwkv_w0': uni(ks[7], (DEPTH, R_WIDTH), -6.0, 1.0),
        'rwkv_w_up': nrm(ks[8], (DEPTH, W_LORA, R_WIDTH), 0.1),
        'rwkv_a0': nrm(ks[9], (DEPTH, R_WIDTH), 0.5),
        'rwkv_a_up': nrm(ks[10], (DEPTH, A_LORA, R_WIDTH), 0.5 * A_LORA ** -0.5),
        'rwkv_g_up': nrm(ks[11], (DEPTH, G_LORA, R_WIDTH), G_LORA ** -0.5),
        'rwkv_k_k': 0.85 + nrm(ks[12], (DEPTH, R_WIDTH), 0.05),
        'rwkv_k_a': 1.0 + nrm(ks[13], (DEPTH, R_WIDTH), 0.05),
        'rwkv_r_k': nrm(ks[14], (DEPTH, R_HEADS, R_HEAD_DIM), 0.1),
        'rwkv_gn_g': 1.0 + nrm(ks[15], (DEPTH, R_WIDTH), 0.02),
        'rwkv_gn_b': nrm(ks[16], (DEPTH, R_WIDTH), 0.02),
        'w_branch_a': nrm(ks[17], (DEPTH, A_Q, D_MODEL), BETA * A_Q ** -0.5),
        'w_branch_b': nrm(ks[18], (DEPTH, R_WIDTH, D_MODEL), BETA * R_WIDTH ** -0.5),
        'w_out': nrm(ks[19], (DEPTH, D_MODEL, D_MODEL), BETA * D_MODEL ** -0.5),
        'ln1_g': 1.0 + nrm(ks[20], (DEPTH, D_MODEL), 0.02),
        'ln1_b': nrm(ks[21], (DEPTH, D_MODEL), 0.02),
        'ln2_g': 1.0 + nrm(ks[22], (DEPTH, D_MODEL), 0.02),
        'ln2_b': nrm(ks[23], (DEPTH, D_MODEL), 0.02),
        'w_router': nrm(ks[24], (D_MODEL, N_EXPERTS), D_MODEL ** -0.5),
        'b_router': nrm(ks[25], (N_EXPERTS,), 0.01),
        'w_exp1': nrm(ks[26], (DEPTH, N_EXPERTS, D_MODEL, D_EXPERT), D_MODEL ** -0.5),
        'w_exp3': nrm(ks[27], (DEPTH, N_EXPERTS, D_MODEL, D_EXPERT), D_MODEL ** -0.5),
        'w_exp2': nrm(ks[28], (DEPTH, N_EXPERTS, D_EXPERT, D_MODEL), BETA * D_EXPERT ** -0.5),
    }


def reference(x, meta_tokens, ln_in_g, ln_in_b, w_in, b_gate, rwkv_mu, rwkv_w0, rwkv_w_up, rwkv_a0, rwkv_a_up,
              rwkv_g_up, rwkv_k_k, rwkv_k_a, rwkv_r_k, rwkv_gn_g, rwkv_gn_b, w_branch_a, w_branch_b, w_out,
              ln1_g, ln1_b, ln2_g, ln2_b, w_router, b_router, w_exp1, w_exp3, w_exp2):
    B, S, D = x.shape
    topk = min(TOPK_MAX, S // 4)
    meta = jnp.broadcast_to(meta_tokens[None].astype(x.dtype), (B, N_META, D))
    h = layer_norm(jnp.concatenate([meta, x], axis=1), ln_in_g, ln_in_b)
    for l in range(DEPTH):
        mix = token_mixers(h, w_in[l], b_gate[l], rwkv_mu[l], rwkv_w0[l], rwkv_w_up[l], rwkv_a0[l], rwkv_a_up[l],
                           rwkv_g_up[l], rwkv_k_k[l], rwkv_k_a[l], rwkv_r_k[l], rwkv_gn_g[l], rwkv_gn_b[l],
                           w_branch_a[l], w_branch_b[l], w_out[l], topk)
        h = layer_norm(ALPHA * h + mix, ln1_g[l], ln1_b[l])
        ffn = grouped_moe(h, w_router, b_router, w_exp1[l], w_exp3[l], w_exp2[l])
        h = layer_norm(ALPHA * h + ffn, ln2_g[l], ln2_b[l])
    return h[:, N_META:]
```

```python
import functools

import jax
import jax.numpy as jnp
from jax import lax
from jax.experimental import pallas as pl
from jax.experimental.pallas import tpu as pltpu

D_MODEL = 1024
CHUNK = 64
N_META = 16
Q_BLOCK = 128
A_HEADS = 8
A_KV_HEADS = 2
A_HEAD_DIM = 64
IDX_HEADS = 8
IDX_DIM = 32
TOPK_MAX = 256
A_Q = A_HEADS * A_HEAD_DIM
A_KV = A_KV_HEADS * A_HEAD_DIM
IDX_Q = IDX_HEADS * IDX_DIM
R_HEADS = 8
R_HEAD_DIM = 64
R_WIDTH = R_HEADS * R_HEAD_DIM
W_LORA = 64
A_LORA = 64
G_LORA = 128
R_COLS = 3 * R_WIDTH + W_LORA + A_LORA + G_LORA
GN_EPS = 64e-5
GATE_COLS = 2 * D_MODEL
N_EXPERTS = 16
N_GROUPS = 4
EXPERTS_PER_GROUP = N_EXPERTS // N_GROUPS
D_EXPERT = 512
LN_EPS = 1e-5

LANES = 128
UA_COLS = A_Q + IDX_Q + 2 * A_KV + LANES
KIW_BLOCK = (A_Q + IDX_Q + 2 * A_KV) // LANES
ROW_TILE = 256
RWKV_A_TILE = 128
RW_CHUNK = 64
VMEM_LIMIT = 56 * 1024 * 1024

F32 = jnp.float32
BF16 = jnp.bfloat16
INT_MIN = -2147483648


def _dot(a, b, dims=(((1,), (0,)), ((), ())), precision=None):
    return lax.dot_general(a, b, dims, precision=precision, preferred_element_type=F32)


def _dot_bf16(a, b, dims=(((1,), (0,)), ((), ()))):
    return _dot(a.astype(BF16), b.astype(BF16), dims)


_NT = (((1,), (1,)), ((), ()))
_TN = (((0,), (0,)), ((), ()))


def _dot_split(x, m_bf16):
    hi = x.astype(BF16)
    lo = (x - hi.astype(F32)).astype(BF16)
    return _dot(hi, m_bf16) + _dot(lo, m_bf16)


def _layer_norm(x, g, b):
    mu = jnp.mean(x, axis=-1, keepdims=True)
    xc = x - mu
    var = jnp.mean(xc * xc, axis=-1, keepdims=True)
    return xc * lax.rsqrt(var + LN_EPS) * g + b


def _params(sem):
    return pltpu.CompilerParams(dimension_semantics=sem, vmem_limit_bytes=VMEM_LIMIT)


def _ln_kernel(x_ref, g_ref, b_ref, o_ref):
    o_ref[...] = _layer_norm(x_ref[...], g_ref[...], b_ref[...])


def _ln_call(x, g, b):
    T, D = x.shape
    return pl.pallas_call(
        _ln_kernel,
        grid=(T // ROW_TILE,),
        in_specs=[pl.BlockSpec((ROW_TILE, D), lambda i: (i, 0)),
                  pl.BlockSpec((1, D), lambda i: (0, 0)),
                  pl.BlockSpec((1, D), lambda i: (0, 0))],
        out_specs=pl.BlockSpec((ROW_TILE, D), lambda i: (i, 0)),
        out_shape=jax.ShapeDtypeStruct((T, D), F32),
        compiler_params=_params(("parallel",)),
        name="ln_in",
    )(x, g.reshape(1, D), b.reshape(1, D))


def _proj_kernel(h_ref, wa_ref, wr_ref, wg_ref, ua_ref, ur_ref, ug_ref):
    hb = h_ref[...].astype(BF16)
    ua_ref[...] = _dot(hb, wa_ref[...])
    ur_ref[...] = _dot(hb, wr_ref[...])
    ug_ref[...] = _dot(hb, wg_ref[...])


def _proj_call(h, wa, wr, wg):
    T, D = h.shape
    row = lambda n: pl.BlockSpec((ROW_TILE, n), lambda i: (i, 0))
    full = lambda n: pl.BlockSpec((D, n), lambda i: (0, 0))
    return pl.pallas_call(
        _proj_kernel,
        grid=(T // ROW_TILE,),
        in_specs=[row(D), full(UA_COLS), full(R_COLS), full(GATE_COLS)],
        out_specs=[row(UA_COLS), row(R_COLS), row(GATE_COLS)],
        out_shape=[jax.ShapeDtypeStruct((T, UA_COLS), F32),
                   jax.ShapeDtypeStruct((T, R_COLS), F32),
                   jax.ShapeDtypeStruct((T, GATE_COLS), F32)],
        compiler_params=_params(("parallel",)),
        name="proj",
    )(h, wa, wr, wg)


def _chunk_id(p):
    return jnp.where(p < N_META, 0, 1 + ((p - N_META) >> 6))


def _dsa_kernel(q_ref, qi_ref, kiw_q_ref, k_ref, v_ref, kiw_ref, o_ref, *, topk, lp):
    i = pl.program_id(1)
    qi = qi_ref[...].astype(BF16)
    wi = kiw_q_ref[:, IDX_DIM:IDX_DIM + IDX_HEADS]
    ki = kiw_ref[:, 0:IDX_DIM].astype(BF16)
    score = jnp.zeros((Q_BLOCK, lp), F32)
    for h in range(IDX_HEADS):
        s = _dot(qi[:, h * IDX_DIM:(h + 1) * IDX_DIM], ki, _NT)
        score = score + wi[:, h:h + 1] * jnp.maximum(s, 0.0)

    qpos = i * Q_BLOCK + lax.broadcasted_iota(jnp.int32, (Q_BLOCK, 1), 0)
    kpos = lax.broadcasted_iota(jnp.int32, (1, lp), 1)
    adm = _chunk_id(kpos) <= _chunk_id(qpos)

    bits = pltpu.bitcast(score, jnp.int32)
    key = bits ^ ((bits >> 31) & 0x7FFFFFFF)
    key = jnp.where(adm, key, INT_MIN)

    def count(mask):
        return jnp.sum(jnp.where(mask, 1.0, 0.0), axis=-1, keepdims=True)

    kf = float(topk)

    def bit_step(it, t_u):
        c_u = t_u | (jnp.int32(1) << (31 - it))
        cnt = count(key >= (c_u ^ INT_MIN))
        return jnp.where(cnt >= kf, c_u, t_u)

    t_u = lax.fori_loop(0, 32, bit_step, jnp.zeros((Q_BLOCK, 1), jnp.int32))
    thr = t_u ^ INT_MIN
    gt = key > thr
    tie = key == thr
    need = kf - count(gt)

    def idx_step(it, j):
        c = j | (jnp.int32(1) << (12 - it))
        cnt = count(jnp.logical_and(tie, kpos < c))
        return jnp.where(cnt <= need, c, j)

    j = lax.fori_loop(0, 13, idx_step, jnp.zeros((Q_BLOCK, 1), jnp.int32))
    sel = jnp.logical_and(adm, jnp.logical_or(gt, jnp.logical_and(tie, kpos < j)))

    q = q_ref[...] * (A_HEAD_DIM ** -0.5)
    group = A_HEADS // A_KV_HEADS
    outs = []
    for h in range(A_HEADS):
        n = h // group
        kn = k_ref[:, n * A_HEAD_DIM:(n + 1) * A_HEAD_DIM].astype(BF16)
        vn = v_ref[:, n * A_HEAD_DIM:(n + 1) * A_HEAD_DIM].astype(BF16)
        qh = q[:, h * A_HEAD_DIM:(h + 1) * A_HEAD_DIM].astype(BF16)
        s = _dot(qh, kn, _NT)
        s = jnp.where(sel, s, -jnp.inf)
        m = jnp.max(s, axis=-1, keepdims=True)
        p = jnp.exp(s - m)
        l = jnp.sum(p, axis=-1, keepdims=True)
        outs.append(_dot(p.astype(BF16), vn) / l)
    o_ref[...] = jnp.concatenate(outs, axis=-1)


def _dsa_call(ua, B, lp, topk):
    nq = lp // Q_BLOCK
    kern = functools.partial(_dsa_kernel, topk=topk, lp=lp)
    qrow = lambda b, i: b * nq + i
    return pl.pallas_call(
        kern,
        grid=(B, nq),
        in_specs=[pl.BlockSpec((Q_BLOCK, A_Q), lambda b, i: (qrow(b, i), 0)),
                  pl.BlockSpec((Q_BLOCK, IDX_Q), lambda b, i: (qrow(b, i), A_Q // IDX_Q)),
                  pl.BlockSpec((Q_BLOCK, LANES), lambda b, i: (qrow(b, i), KIW_BLOCK)),
                  pl.BlockSpec((lp, A_KV), lambda b, i: (b, (A_Q + IDX_Q) // A_KV)),
                  pl.BlockSpec((lp, A_KV), lambda b, i: (b, (A_Q + IDX_Q) // A_KV + 1)),
                  pl.BlockSpec((lp, LANES), lambda b, i: (b, KIW_BLOCK))],
        out_specs=pl.BlockSpec((Q_BLOCK, A_Q), lambda b, i: (qrow(b, i), 0)),
        out_shape=jax.ShapeDtypeStruct((B * lp, A_Q), F32),
        compiler_params=_params(("parallel", "parallel")),
        name="dsa",
    )(ua, ua, ua, ua, ua, ua)


def _rwkv_a_kernel(ur_ref, mu_ref, w0_ref, wup_ref, a0_ref, aup_ref, gup_ref, kk_ref, ka_ref, rk_ref,
                   tril_ref, hsum_ref,
                   at_ref, bt_ref, kt_ref, rt_ref, v_ref, gc_ref, bonus_ref, g_ref, prev_ref):
    t = pl.program_id(1)
    tm = ur_ref.shape[0]

    @pl.when(t == 0)
    def _():
        prev_ref[...] = jnp.zeros_like(prev_ref)

    u = ur_ref[...]
    rows = lax.broadcasted_iota(jnp.int32, (tm, 1), 0)
    u_prev = jnp.where(rows == 0, prev_ref[0:1, :], pltpu.roll(u, 1, 0))
    prev_ref[0:1, :] = u[tm - 1:tm, :]
    u = u + (u_prev - u) * mu_ref[...]

    W = R_WIDTH
    r, k, v = u[:, 0:W], u[:, W:2 * W], u[:, 2 * W:3 * W]
    xw = u[:, 3 * W:3 * W + W_LORA]
    xa = u[:, 3 * W + W_LORA:3 * W + W_LORA + A_LORA]
    xg = u[:, 3 * W + W_LORA + A_LORA:]

    hp = lax.Precision.HIGHEST
    zw = w0_ref[...] + _dot(jnp.tanh(xw), wup_ref[...], precision=hp)
    logw = -(jnp.maximum(-zw, 0.0) + jnp.log1p(jnp.exp(-jnp.abs(zw)))) - 0.5
    lw = -jnp.exp(logw)
    a = jax.nn.sigmoid(a0_ref[...] + _dot(xa, aup_ref[...], precision=hp))
    g = _dot(jax.nn.sigmoid(xg), gup_ref[...], precision=hp)

    hsum = hsum_ref[...]
    kk = k * kk_ref[...]
    kk = kk * lax.rsqrt(jnp.maximum(_dot_split(kk * kk, hsum), 1e-24))
    kmod = k * (1.0 + (a - 1.0) * ka_ref[...])
    bonus_ref[...] = _dot_split(r * kmod * rk_ref[...], hsum) * v
    g_ref[...] = g
    v_ref[...] = v

    cs = _dot(tril_ref[...], lw, precision=hp)
    e_neg = jnp.exp(-cs)
    at_ref[...] = -kk * jnp.exp(cs - lw)
    bt_ref[...] = kk * a * e_neg
    kt_ref[...] = kmod * e_neg
    rt_ref[...] = r * jnp.exp(cs)
    for c in range(tm // RW_CHUNK):
        gc_ref[c] = jnp.exp(cs[(c + 1) * RW_CHUNK - 1:(c + 1) * RW_CHUNK, :])


def _rwkv_a_call(ur, B, lp, mu, w0, w_up, a0, a_up, g_up, k_k, k_a, r_k):
    T = B * lp
    tm = RWKV_A_TILE
    nt = lp // tm
    W = R_WIDTH
    ri = lax.broadcasted_iota(jnp.int32, (tm, tm), 0)
    ci = lax.broadcasted_iota(jnp.int32, (tm, tm), 1)
    tril = jnp.where(jnp.logical_and(ri >= ci, ri // RW_CHUNK == ci // RW_CHUNK), 1.0, 0.0).astype(F32)
    hr = lax.broadcasted_iota(jnp.int32, (W, W), 0) // R_HEAD_DIM
    hc = lax.broadcasted_iota(jnp.int32, (W, W), 1) // R_HEAD_DIM
    hsum = jnp.where(hr == hc, 1.0, 0.0).astype(BF16)
    vec = lambda x: x.reshape(1, -1)
    const = lambda shape: pl.BlockSpec(shape, lambda b, t: (0,) * len(shape))
    rowspec = lambda n: pl.BlockSpec((tm, n), lambda b, t: (b * nt + t, 0))
    nck = tm // RW_CHUNK
    out_rows = jax.ShapeDtypeStruct((T, W), F32)
    return pl.pallas_call(
        _rwkv_a_kernel,
        grid=(B, nt),
        in_specs=[rowspec(R_COLS), const((1, R_COLS)), const((1, W)), const((W_LORA, W)), const((1, W)),
                  const((A_LORA, W)), const((G_LORA, W)), const((1, W)), const((1, W)), const((1, W)),
                  const((tm, tm)), const((W, W))],
        out_specs=[rowspec(W), rowspec(W), rowspec(W), rowspec(W), rowspec(W),
                   pl.BlockSpec((nck, 1, W), lambda b, t: (b * nt + t, 0, 0)),
                   rowspec(W), rowspec(W)],
        out_shape=[out_rows, out_rows, out_rows, out_rows, out_rows,
                   jax.ShapeDtypeStruct((T // RW_CHUNK, 1, W), F32), out_rows, out_rows],
        scratch_shapes=[pltpu.VMEM((8, R_COLS), F32)],
        compiler_params=_params(("parallel", "arbitrary")),
        name="rwkv_a",
    )(ur, vec(mu), vec(w0), w_up, vec(a0), a_up, g_up, vec(k_k), vec(k_a), vec(r_k), tril, hsum)


def _rwkv_b_kernel(at_ref, bt_ref, kt_ref, rt_ref, v_ref, gc_ref, y_ref, s_ref, *, precision):
    c = pl.program_id(1)

    @pl.when(c == 0)
    def _():
        s_ref[...] = jnp.zeros_like(s_ref)

    C = RW_CHUNK
    P = 2 * C
    lane = lax.broadcasted_iota(jnp.int32, (C, P), 1)
    lo = lane < R_HEAD_DIM
    ri = lax.broadcasted_iota(jnp.int32, (P, P), 0)
    ci = lax.broadcasted_iota(jnp.int32, (P, P), 1)
    same = (ri // C) == (ci // C)
    strict = jnp.logical_and(same, ri > ci)
    incl = jnp.logical_and(same, ri >= ci)
    eye = jnp.where(ri == ci, 1.0, 0.0).astype(F32)
    mm = functools.partial(_dot, precision=precision)

    def stack(x):
        return jnp.concatenate([jnp.where(lo, x, 0.0), jnp.where(lo, 0.0, x)], axis=0)

    for p in range(R_HEADS // 2):
        sl = slice(p * P, (p + 1) * P)
        a_s, b_s, k_s = stack(at_ref[:, sl]), stack(bt_ref[:, sl]), stack(kt_ref[:, sl])
        r_s, v_s = stack(rt_ref[:, sl]), stack(v_ref[:, sl])
        g = mm(jnp.concatenate([a_s, r_s], axis=0), jnp.concatenate([b_s, k_s], axis=0), _NT)
        l_ab = jnp.where(strict, g[0:P, 0:P], 0.0)
        l_ak = jnp.where(strict, g[0:P, P:2 * P], 0.0)
        m_rb = jnp.where(incl, g[P:2 * P, 0:P], 0.0)
        m_rk = jnp.where(incl, g[P:2 * P, P:2 * P], 0.0)
        t_inv = eye + l_ab
        pw = l_ab
        for _ in range(5):
            pw = mm(pw, pw)
            t_inv = t_inv + mm(t_inv, pw)
        s0 = s_ref[p]
        z = mm(a_s, s0, _NT) + mm(l_ak, v_s)
        u = mm(t_inv, z)
        y_s = mm(r_s, s0, _NT) + mm(m_rb, u) + mm(m_rk, v_s)
        y_ref[:, sl] = y_s[0:C, :] + y_s[C:P, :]
        s_ref[p] = (s0 + mm(u, b_s, _TN) + mm(v_s, k_s, _TN)) * gc_ref[0, :, sl]


def _rwkv_b_call(at, bt, kt, rt, v, gc, B, lp, precision):
    T, W = at.shape
    nc = lp // RW_CHUNK
    rowspec = pl.BlockSpec((RW_CHUNK, W), lambda b, c: (b * nc + c, 0))
    return pl.pallas_call(
        functools.partial(_rwkv_b_kernel, precision=precision),
        grid=(B, nc),
        in_specs=[rowspec, rowspec, rowspec, rowspec, rowspec,
                  pl.BlockSpec((1, 1, W), lambda b, c: (b * nc + c, 0, 0))],
        out_specs=rowspec,
        out_shape=jax.ShapeDtypeStruct((T, W), F32),
        scratch_shapes=[pltpu.VMEM((R_HEADS // 2, 2 * RW_CHUNK, 2 * R_HEAD_DIM), F32)],
        compiler_params=_params(("parallel", "arbitrary")),
        name="rwkv_b",
    )(at, bt, kt, rt, v, gc)


def _merge_kernel(h_ref, oa_ref, y_ref, bonus_ref, g_ref, ug_ref, bg_ref, gng_ref, gnb_ref, hmean_ref,
                  wa_ref, wb_ref, wo_ref, lng_ref, lnb_ref, o_ref, *, alpha):
    y = y_ref[...]
    hmean = hmean_ref[...]
    m = _dot_split(y, hmean)
    yc = y - m
    var = _dot_split(yc * yc, hmean)
    yn = yc * lax.rsqrt(var + GN_EPS) * gng_ref[...] + gnb_ref[...]
    ob = (yn + bonus_ref[...]) * g_ref[...]
    gates = jax.nn.sigmoid(ug_ref[...] + bg_ref[...])
    merged = (gates[:, 0:D_MODEL] * _dot_bf16(oa_ref[...], wa_ref[...])
              + gates[:, D_MODEL:] * _dot_bf16(ob, wb_ref[...]))
    mix = _dot_bf16(merged, wo_ref[...])
    o_ref[...] = _layer_norm(alpha * h_ref[...] + mix, lng_ref[...], lnb_ref[...])


def _merge_call(h, oa, y, bonus, g, ug, b_gate, gn_g, gn_b, wa, wb, wo, ln_g, ln_b, alpha):
    T, D = h.shape
    W = R_WIDTH
    hr = lax.broadcasted_iota(jnp.int32, (W, W), 0) // R_HEAD_DIM
    hc = lax.broadcasted_iota(jnp.int32, (W, W), 1) // R_HEAD_DIM
    hmean = jnp.where(hr == hc, 1.0 / R_HEAD_DIM, 0.0).astype(BF16)
    vec = lambda x: x.reshape(1, -1)
    row = lambda n: pl.BlockSpec((ROW_TILE, n), lambda i: (i, 0))
    const = lambda a, b: pl.BlockSpec((a, b), lambda i: (0, 0))
    return pl.pallas_call(
        functools.partial(_merge_kernel, alpha=alpha),
        grid=(T // ROW_TILE,),
        in_specs=[row(D), row(A_Q), row(W), row(W), row(W), row(GATE_COLS), const(1, GATE_COLS),
                  const(1, W), const(1, W), const(W, W), const(A_Q, D), const(W, D), const(D, D),
                  const(1, D), const(1, D)],
        out_specs=row(D),
        out_shape=jax.ShapeDtypeStruct((T, D), F32),
        compiler_params=_params(("parallel",)),
        name="merge",
    )(h, oa, y, bonus, g, ug, vec(b_gate), vec(gn_g), vec(gn_b), hmean,
      wa.astype(BF16), wb.astype(BF16), wo.astype(BF16), vec(ln_g), vec(ln_b))


MOE_TILES = (1536, 1024, 768, 512, 256)


def _route(logits_t, bias_t):
    s = jax.nn.sigmoid(logits_t)
    sel = s + bias_t
    row = lambda a, e: a[e:e + 1, :]
    gscore = []
    for gi in range(N_GROUPS):
        a, b, c, d = (row(sel, gi * EXPERTS_PER_GROUP + j) for j in range(EXPERTS_PER_GROUP))
        hi1, lo1 = jnp.maximum(a, b), jnp.minimum(a, b)
        hi2, lo2 = jnp.maximum(c, d), jnp.minimum(c, d)
        top1 = jnp.maximum(hi1, hi2)
        top2 = jnp.maximum(jnp.minimum(hi1, hi2), jnp.maximum(lo1, lo2))
        gscore.append(top1 + top2)
    chosen = []
    for gi in range(N_GROUPS):
        ok = None
        for gj in range(N_GROUPS):
            if gj == gi:
                continue
            t = gscore[gi] > gscore[gj] if gj < gi else gscore[gi] >= gscore[gj]
            ok = t if ok is None else jnp.logical_and(ok, t)
        chosen.append(ok)
    picked = []
    for e in range(N_EXPERTS):
        gi = e // EXPERTS_PER_GROUP
        rank = jnp.zeros_like(row(sel, e))
        for e2 in range(gi * EXPERTS_PER_GROUP, (gi + 1) * EXPERTS_PER_GROUP):
            if e2 == e:
                continue
            ahead = row(sel, e2) > row(sel, e) if e2 > e else row(sel, e2) >= row(sel, e)
            rank = rank + jnp.where(ahead, 1.0, 0.0)
        take = jnp.logical_and(chosen[gi], rank < 2.0)
        picked.append(jnp.where(take, row(s, e), 0.0))
    total = picked[0]
    for e in range(1, N_EXPERTS):
        total = total + picked[e]
    return jnp.concatenate(picked, axis=0) / total


def _moe_kernel(h_ref, wr_ref, br_ref, w1_ref, w3_ref, w2_ref, lng_ref, lnb_ref, o_ref,
                xb_ref, comb_ref, acc_ref, *, alpha):
    e = pl.program_id(1)

    @pl.when(e == 0)
    def _():
        x = h_ref[...]
        xb_ref[...] = x.astype(BF16)
        logits_t = _dot(wr_ref[...], x, _NT, precision=lax.Precision.HIGHEST)
        comb_t = _route(logits_t, br_ref[...])
        pad = jnp.zeros((LANES - N_EXPERTS, comb_t.shape[1]), F32)
        comb_ref[...] = jnp.transpose(jnp.concatenate([comb_t, pad], axis=0))
        acc_ref[...] = jnp.zeros_like(acc_ref)

    xb = xb_ref[...]
    h1 = _dot(xb, w1_ref[0])
    h3 = _dot(xb, w3_ref[0])
    act = (h1 * jax.nn.sigmoid(h1)) * h3
    he = _dot(act.astype(BF16), w2_ref[0])
    lane = lax.broadcasted_iota(jnp.int32, comb_ref.shape, 1)
    ce = jnp.sum(jnp.where(lane == e, comb_ref[...], 0.0), axis=-1, keepdims=True)
    acc_ref[...] += ce * he

    @pl.when(e == pl.num_programs(1) - 1)
    def _():
        o_ref[...] = _layer_norm(alpha * h_ref[...] + acc_ref[...], lng_ref[...], lnb_ref[...])


def _moe_call(h, w_router_t, b_router, w1, w3, w2, ln_g, ln_b, alpha, tile):
    T, D = h.shape
    vec = lambda x: x.reshape(1, -1)
    return pl.pallas_call(
        functools.partial(_moe_kernel, alpha=alpha),
        grid=(T // tile, N_EXPERTS),
        in_specs=[pl.BlockSpec((tile, D), lambda i, e: (i, 0)),
                  pl.BlockSpec((N_EXPERTS, D), lambda i, e: (0, 0)),
                  pl.BlockSpec((N_EXPERTS, 1), lambda i, e: (0, 0)),
                  pl.BlockSpec((1, D, D_EXPERT), lambda i, e: (e, 0, 0)),
                  pl.BlockSpec((1, D, D_EXPERT), lambda i, e: (e, 0, 0)),
                  pl.BlockSpec((1, D_EXPERT, D), lambda i, e: (e, 0, 0)),
                  pl.BlockSpec((1, D), lambda i, e: (0, 0)),
                  pl.BlockSpec((1, D), lambda i, e: (0, 0))],
        out_specs=pl.BlockSpec((tile, D), lambda i, e: (i, 0)),
        out_shape=jax.ShapeDtypeStruct((T, D), F32),
        scratch_shapes=[pltpu.VMEM((tile, D), BF16), pltpu.VMEM((tile, LANES), F32),
                        pltpu.VMEM((tile, D), F32)],
        compiler_params=_params(("parallel", "arbitrary")),
        name="moe",
    )(h, w_router_t, b_router.reshape(N_EXPERTS, 1), w1, w3, w2, vec(ln_g), vec(ln_b))


def _split_w_in(w):
    o = 0
    q = w[:, o:o + A_Q]; o += A_Q
    k = w[:, o:o + A_KV]; o += A_KV
    v = w[:, o:o + A_KV]; o += A_KV
    qi = w[:, o:o + IDX_Q]; o += IDX_Q
    kiw = w[:, o:o + IDX_DIM + IDX_HEADS]; o += IDX_DIM + IDX_HEADS
    pad = jnp.zeros((w.shape[0], LANES - IDX_DIM - IDX_HEADS), w.dtype)
    wa = jnp.concatenate([q, qi, k, v, kiw, pad], axis=1)
    wr = w[:, o:o + R_COLS]; o += R_COLS
    wg = w[:, o:o + GATE_COLS]
    return wa.astype(BF16), wr.astype(BF16), wg.astype(BF16)


def _padded_len(L):
    return -(-L // Q_BLOCK) * Q_BLOCK


def kernel(x, meta_tokens, ln_in_g, ln_in_b, w_in, b_gate, rwkv_mu, rwkv_w0, rwkv_w_up, rwkv_a0, rwkv_a_up,
           rwkv_g_up, rwkv_k_k, rwkv_k_a, rwkv_r_k, rwkv_gn_g, rwkv_gn_b, w_branch_a, w_branch_b, w_out,
           ln1_g, ln1_b, ln2_g, ln2_b, w_router, b_router, w_exp1, w_exp3, w_exp2):
    B, S, D = x.shape
    depth = w_in.shape[0]
    alpha = (2 * depth) ** 0.25
    topk = min(TOPK_MAX, S // 4)
    L = N_META + S
    lp = _padded_len(L)
    T = B * lp
    assert T % ROW_TILE == 0
    moe_tile = next(t for t in MOE_TILES if T % t == 0)
    meta = jnp.broadcast_to(meta_tokens[None].astype(x.dtype), (B, N_META, D))
    hin = jnp.concatenate([meta, x, jnp.zeros((B, lp - L, D), x.dtype)], axis=1).reshape(T, D)
    h = _ln_call(hin, ln_in_g, ln_in_b)
    w_router_t = jnp.transpose(w_router)
    for l in range(depth):
        wa, wr, wg = _split_w_in(w_in[l])
        ua, ur, ug = _proj_call(h, wa, wr, wg)
        oa = _dsa_call(ua, B, lp, topk)
        at, bt, kt, rt, v, gc, bonus, g = _rwkv_a_call(
            ur, B, lp, rwkv_mu[l], rwkv_w0[l], rwkv_w_up[l], rwkv_a0[l], rwkv_a_up[l], rwkv_g_up[l],
            rwkv_k_k[l], rwkv_k_a[l], rwkv_r_k[l])
        y = _rwkv_b_call(at, bt, kt, rt, v, gc, B, lp, lax.Precision.HIGHEST)
        h = _merge_call(h, oa, y, bonus, g, ug, b_gate[l], rwkv_gn_g[l], rwkv_gn_b[l],
                        w_branch_a[l], w_branch_b[l], w_out[l], ln1_g[l], ln1_b[l], alpha)
        h = _moe_call(h, w_router_t, b_router, w_exp1[l].astype(BF16), w_exp3[l].astype(BF16),
                      w_exp2[l].astype(BF16), ln2_g[l], ln2_b[l], alpha, moe_tile)
    return h.reshape(B, lp, D)[:, N_META:L]
```

```python
import functools

import jax
import jax.numpy as jnp
from jax import lax
from jax.experimental import pallas as pl
from jax.experimental.pallas import tpu as pltpu

D_MODEL = 1024
CHUNK = 64
N_META = 16
Q_BLOCK = 128
A_HEADS = 8
A_KV_HEADS = 2
A_HEAD_DIM = 64
IDX_HEADS = 8
IDX_DIM = 32
TOPK_MAX = 256
A_Q = A_HEADS * A_HEAD_DIM
A_KV = A_KV_HEADS * A_HEAD_DIM
IDX_Q = IDX_HEADS * IDX_DIM
R_HEADS = 8
R_HEAD_DIM = 64
R_WIDTH = R_HEADS * R_HEAD_DIM
W_LORA = 64
A_LORA = 64
G_LORA = 128
R_COLS = 3 * R_WIDTH + W_LORA + A_LORA + G_LORA
GN_EPS = 64e-5
GATE_COLS = 2 * D_MODEL
N_EXPERTS = 16
N_GROUPS = 4
EXPERTS_PER_GROUP = N_EXPERTS // N_GROUPS
D_EXPERT = 512
LN_EPS = 1e-5

LANES = 128
UA_COLS = A_Q + IDX_Q + 2 * A_KV + LANES
KIW_BLOCK = (A_Q + IDX_Q + 2 * A_KV) // LANES
ROW_TILE = 256
RWKV_A_TILE = 128
RW_CHUNK = 64
VMEM_LIMIT = 56 * 1024 * 1024

F32 = jnp.float32
BF16 = jnp.bfloat16
INT_MIN = -2147483648


def _dot(a, b, dims=(((1,), (0,)), ((), ())), precision=None):
    return lax.dot_general(a, b, dims, precision=precision, preferred_element_type=F32)


def _dot_bf16(a, b, dims=(((1,), (0,)), ((), ()))):
    return _dot(a.astype(BF16), b.astype(BF16), dims)


_NT = (((1,), (1,)), ((), ()))
_TN = (((0,), (0,)), ((), ()))


def _dot_split(x, m_bf16):
    hi = x.astype(BF16)
    lo = (x - hi.astype(F32)).astype(BF16)
    return _dot(hi, m_bf16) + _dot(lo, m_bf16)


def _layer_norm(x, g, b):
    mu = jnp.mean(x, axis=-1, keepdims=True)
    xc = x - mu
    var = jnp.mean(xc * xc, axis=-1, keepdims=True)
    return xc * lax.rsqrt(var + LN_EPS) * g + b


def _params(sem):
    return pltpu.CompilerParams(dimension_semantics=sem, vmem_limit_bytes=VMEM_LIMIT)


def _ln_kernel(x_ref, g_ref, b_ref, o_ref):
    o_ref[...] = _layer_norm(x_ref[...], g_ref[...], b_ref[...])


def _ln_call(x, g, b):
    T, D = x.shape
    return pl.pallas_call(
        _ln_kernel,
        grid=(T // ROW_TILE,),
        in_specs=[pl.BlockSpec((ROW_TILE, D), lambda i: (i, 0)),
                  pl.BlockSpec((1, D), lambda i: (0, 0)),
                  pl.BlockSpec((1, D), lambda i: (0, 0))],
        out_specs=pl.BlockSpec((ROW_TILE, D), lambda i: (i, 0)),
        out_shape=jax.ShapeDtypeStruct((T, D), F32),
        compiler_params=_params(("parallel",)),
        name="ln_in",
    )(x, g.reshape(1, D), b.reshape(1, D))


def _proj_kernel(h_ref, wa_ref, wr_ref, wg_ref, ua_ref, ur_ref, ug_ref):
    hb = h_ref[...].astype(BF16)
    ua_ref[...] = _dot(hb, wa_ref[...])
    ur_ref[...] = _dot(hb, wr_ref[...])
    ug_ref[...] = _dot(hb, wg_ref[...])


def _proj_call(h, wa, wr, wg):
    T, D = h.shape
    row = lambda n: pl.BlockSpec((ROW_TILE, n), lambda i: (i, 0))
    full = lambda n: pl.BlockSpec((D, n), lambda i: (0, 0))
    return pl.pallas_call(
        _proj_kernel,
        grid=(T // ROW_TILE,),
        in_specs=[row(D), full(UA_COLS), full(R_COLS), full(GATE_COLS)],
        out_specs=[row(UA_COLS), row(R_COLS), row(GATE_COLS)],
        out_shape=[jax.ShapeDtypeStruct((T, UA_COLS), F32),
                   jax.ShapeDtypeStruct((T, R_COLS), F32),
                   jax.ShapeDtypeStruct((T, GATE_COLS), F32)],
        compiler_params=_params(("parallel",)),
        name="proj",
    )(h, wa, wr, wg)


assert CHUNK & (CHUNK - 1) == 0
CHUNK_SHIFT = CHUNK.bit_length() - 1
BISECT_STEPS = 24
REFINE_CAP = 256
KEY_EXTENT_FRACTIONS = (0.09, 0.18, 0.3, 0.48, 0.73, 1.0)


def _chunk_id(p):
    return jnp.where(p < N_META, 0, 1 + ((p - N_META) >> CHUNK_SHIFT))


def _count(mask):
    return jnp.sum(jnp.where(mask, 1.0, 0.0), axis=-1, keepdims=True)


def _any(mask):
    return jnp.max(jnp.where(mask, 1.0, 0.0)) > 0.5


def _topk_select(sm, kpos, n_adm, topk):
    kf = float(topk)
    mx = jnp.max(sm, axis=-1, keepdims=True)
    mn = jnp.min(jnp.where(sm == -jnp.inf, jnp.inf, sm), axis=-1, keepdims=True)
    hi0 = jnp.where(mx >= 0.0, 2.0 * mx + 1.0, 0.5 * mx + 1.0)

    def bisect(st):
        lo, hi, clo, chi = st
        mid = 0.5 * (lo + hi)
        cnt = _count(sm >= mid)
        ge = cnt >= kf
        return (jnp.where(ge, mid, lo), jnp.where(ge, hi, mid), jnp.where(ge, cnt, clo), jnp.where(ge, chi, cnt))

    st = lax.fori_loop(0, BISECT_STEPS, lambda _, s: bisect(s), (mn, hi0, n_adm, jnp.zeros_like(mn)))

    def open_rows(lo, hi, clo):
        band = jnp.logical_and(sm >= lo, sm < hi)
        bmax = jnp.max(jnp.where(band, sm, -jnp.inf), axis=-1, keepdims=True)
        bmin = jnp.min(jnp.where(band, sm, jnp.inf), axis=-1, keepdims=True)
        return _any(jnp.logical_and(clo > kf, bmax != bmin)).astype(jnp.int32)

    def refine(st):
        lo, hi, clo, chi = st

        def body(c):
            lo, hi, clo, chi = bisect(c[0:4])
            return lo, hi, clo, chi, open_rows(lo, hi, clo), c[5] + 1

        lo, hi, clo, chi, _, _ = lax.while_loop(
            lambda c: jnp.logical_and(c[4] > 0, c[5] < REFINE_CAP), body,
            (lo, hi, clo, chi, open_rows(lo, hi, clo), jnp.int32(0)))
        need = kf - chi
        band = jnp.logical_and(sm >= lo, sm < hi)
        nbits = int(sm.shape[1] - 1).bit_length()

        def idx_step(it, j):
            c = j | (jnp.int32(1) << (nbits - 1 - it))
            cnt = _count(jnp.logical_and(band, kpos < c))
            return jnp.where(cnt <= need, c, j)

        j = lax.fori_loop(0, nbits, idx_step, jnp.zeros(lo.shape, jnp.int32))
        return lo, hi, jnp.where(clo > kf, j, jnp.int32(2 ** 30))

    lo, hi, clo, chi = st
    lo, hi, j = lax.cond(_any(clo > kf), refine,
                         lambda s: (s[0], s[1], jnp.full(s[0].shape, 2 ** 30, jnp.int32)), st)
    return jnp.logical_or(sm >= hi, jnp.logical_and(sm >= lo, kpos < j))


def _dsa_block(q_ref, qi_ref, kiw_q_ref, k_ref, v_ref, kiw_ref, o_ref, *, i, topk, nkeys):
    qi = qi_ref[...].astype(BF16)
    wi = kiw_q_ref[:, IDX_DIM:IDX_DIM + IDX_HEADS]
    ki = kiw_ref[0:nkeys, 0:IDX_DIM].astype(BF16)
    score = jnp.zeros((Q_BLOCK, nkeys), F32)
    for h in range(IDX_HEADS):
        s = _dot(qi[:, h * IDX_DIM:(h + 1) * IDX_DIM], ki, _NT)
        score = score + wi[:, h:h + 1] * jnp.maximum(s, 0.0)

    qpos = i * Q_BLOCK + lax.broadcasted_iota(jnp.int32, (Q_BLOCK, 1), 0)
    kpos = lax.broadcasted_iota(jnp.int32, (1, nkeys), 1)
    adm = _chunk_id(kpos) <= _chunk_id(qpos)
    sm = jnp.where(adm, score, -jnp.inf)
    sel = _topk_select(sm, kpos, _count(adm), topk)

    q = q_ref[...] * (A_HEAD_DIM ** -0.5)
    group = A_HEADS // A_KV_HEADS
    outs = []
    for h in range(A_HEADS):
        n = h // group
        kn = k_ref[0:nkeys, n * A_HEAD_DIM:(n + 1) * A_HEAD_DIM].astype(BF16)
        vn = v_ref[0:nkeys, n * A_HEAD_DIM:(n + 1) * A_HEAD_DIM].astype(BF16)
        qh = q[:, h * A_HEAD_DIM:(h + 1) * A_HEAD_DIM].astype(BF16)
        s = _dot(qh, kn, _NT)
        s = jnp.where(sel, s, -jnp.inf)
        m = jnp.max(s, axis=-1, keepdims=True)
        p = jnp.exp(s - m)
        l = jnp.sum(p, axis=-1, keepdims=True)
        outs.append(_dot(p.astype(BF16), vn) / l)
    o_ref[...] = jnp.concatenate(outs, axis=-1)


def _key_extents(nq):
    return sorted({min(nq, max(1, round(nq * f))) for f in KEY_EXTENT_FRACTIONS} | {nq})


def _dsa_kernel(q_ref, qi_ref, kiw_q_ref, k_ref, v_ref, kiw_ref, o_ref, *, topk, nq):
    i = pl.program_id(1)
    last_chunk = 1 + ((i + 1) * Q_BLOCK - 1 - N_META) // CHUNK
    keys_end = N_META + CHUNK * last_chunk
    need = jnp.minimum((keys_end + Q_BLOCK - 1) // Q_BLOCK, nq)
    prev = 0
    for e in _key_extents(nq):
        @pl.when(jnp.logical_and(need > prev, need <= e))
        def _(e=e):
            _dsa_block(q_ref, qi_ref, kiw_q_ref, k_ref, v_ref, kiw_ref, o_ref, i=i, topk=topk, nkeys=e * Q_BLOCK)
        prev = e


def _dsa_call(ua, B, lp, topk):
    nq = lp // Q_BLOCK
    kern = functools.partial(_dsa_kernel, topk=topk, nq=nq)
    qrow = lambda b, i: b * nq + i
    return pl.pallas_call(
        kern,
        grid=(B, nq),
        in_specs=[pl.BlockSpec((Q_BLOCK, A_Q), lambda b, i: (qrow(b, i), 0)),
                  pl.BlockSpec((Q_BLOCK, IDX_Q), lambda b, i: (qrow(b, i), A_Q // IDX_Q)),
                  pl.BlockSpec((Q_BLOCK, LANES), lambda b, i: (qrow(b, i), KIW_BLOCK)),
                  pl.BlockSpec((lp, A_KV), lambda b, i: (b, (A_Q + IDX_Q) // A_KV)),
                  pl.BlockSpec((lp, A_KV), lambda b, i: (b, (A_Q + IDX_Q) // A_KV + 1)),
                  pl.BlockSpec((lp, LANES), lambda b, i: (b, KIW_BLOCK))],
        out_specs=pl.BlockSpec((Q_BLOCK, A_Q), lambda b, i: (qrow(b, i), 0)),
        out_shape=jax.ShapeDtypeStruct((B * lp, A_Q), F32),
        compiler_params=_params(("parallel", "parallel")),
        name="dsa",
    )(ua, ua, ua, ua, ua, ua)


def _rwkv_a_kernel(ur_ref, mu_ref, w0_ref, wup_ref, a0_ref, aup_ref, gup_ref, kk_ref, ka_ref, rk_ref,
                   tril_ref, hsum_ref,
                   at_ref, bt_ref, kt_ref, rt_ref, v_ref, gc_ref, bonus_ref, g_ref, prev_ref):
    t = pl.program_id(1)
    tm = ur_ref.shape[0]

    @pl.when(t == 0)
    def _():
        prev_ref[...] = jnp.zeros_like(prev_ref)

    u = ur_ref[...]
    rows = lax.broadcasted_iota(jnp.int32, (tm, 1), 0)
    u_prev = jnp.where(rows == 0, prev_ref[0:1, :], pltpu.roll(u, 1, 0))
    prev_ref[0:1, :] = u[tm - 1:tm, :]
    u = u + (u_prev - u) * mu_ref[...]

    W = R_WIDTH
    r, k, v = u[:, 0:W], u[:, W:2 * W], u[:, 2 * W:3 * W]
    xw = u[:, 3 * W:3 * W + W_LORA]
    xa = u[:, 3 * W + W_LORA:3 * W + W_LORA + A_LORA]
    xg = u[:, 3 * W + W_LORA + A_LORA:]

    hp = lax.Precision.HIGHEST
    zw = w0_ref[...] + _dot(jnp.tanh(xw), wup_ref[...], precision=hp)
    logw = -(jnp.maximum(-zw, 0.0) + jnp.log1p(jnp.exp(-jnp.abs(zw)))) - 0.5
    lw = -jnp.exp(logw)
    a = jax.nn.sigmoid(a0_ref[...] + _dot(xa, aup_ref[...], precision=hp))
    g = _dot(jax.nn.sigmoid(xg), gup_ref[...], precision=hp)

    hsum = hsum_ref[...]
    kk = k * kk_ref[...]
    kk = kk * lax.rsqrt(jnp.maximum(_dot_split(kk * kk, hsum), 1e-24))
    kmod = k * (1.0 + (a - 1.0) * ka_ref[...])
    bonus_ref[...] = _dot_split(r * kmod * rk_ref[...], hsum) * v
    g_ref[...] = g
    v_ref[...] = v.astype(v_ref.dtype)

    cs = _dot(tril_ref[...], lw, precision=hp)
    e_neg = jnp.exp(-cs)
    at_ref[...] = (-kk * jnp.exp(cs - lw)).astype(at_ref.dtype)
    bt_ref[...] = (kk * a * e_neg).astype(bt_ref.dtype)
    kt_ref[...] = (kmod * e_neg).astype(kt_ref.dtype)
    rt_ref[...] = (r * jnp.exp(cs)).astype(rt_ref.dtype)
    for c in range(tm // RW_CHUNK):
        gc_ref[c] = jnp.exp(cs[(c + 1) * RW_CHUNK - 1:(c + 1) * RW_CHUNK, :])


def _rwkv_a_call(ur, B, lp, mu, w0, w_up, a0, a_up, g_up, k_k, k_a, r_k):
    T = B * lp
    tm = RWKV_A_TILE
    nt = lp // tm
    W = R_WIDTH
    ri = lax.broadcasted_iota(jnp.int32, (tm, tm), 0)
    ci = lax.broadcasted_iota(jnp.int32, (tm, tm), 1)
    tril = jnp.where(jnp.logical_and(ri >= ci, ri // RW_CHUNK == ci // RW_CHUNK), 1.0, 0.0).astype(F32)
    hr = lax.broadcasted_iota(jnp.int32, (W, W), 0) // R_HEAD_DIM
    hc = lax.broadcasted_iota(jnp.int32, (W, W), 1) // R_HEAD_DIM
    hsum = jnp.where(hr == hc, 1.0, 0.0).astype(BF16)
    vec = lambda x: x.reshape(1, -1)
    const = lambda shape: pl.BlockSpec(shape, lambda b, t: (0,) * len(shape))
    rowspec = lambda n: pl.BlockSpec((tm, n), lambda b, t: (b * nt + t, 0))
    nck = tm // RW_CHUNK
    out_rows = jax.ShapeDtypeStruct((T, W), F32)
    mxu_rows = jax.ShapeDtypeStruct((T, W), BF16)
    return pl.pallas_call(
        _rwkv_a_kernel,
        grid=(B, nt),
        in_specs=[rowspec(R_COLS), const((1, R_COLS)), const((1, W)), const((W_LORA, W)), const((1, W)),
                  const((A_LORA, W)), const((G_LORA, W)), const((1, W)), const((1, W)), const((1, W)),
                  const((tm, tm)), const((W, W))],
        out_specs=[rowspec(W), rowspec(W), rowspec(W), rowspec(W), rowspec(W),
                   pl.BlockSpec((nck, 1, W), lambda b, t: (b * nt + t, 0, 0)),
                   rowspec(W), rowspec(W)],
        out_shape=[mxu_rows, mxu_rows, mxu_rows, mxu_rows, mxu_rows,
                   jax.ShapeDtypeStruct((T // RW_CHUNK, 1, W), F32), out_rows, out_rows],
        scratch_shapes=[pltpu.VMEM((8, R_COLS), F32)],
        compiler_params=_params(("parallel", "arbitrary")),
        name="rwkv_a",
    )(ur, vec(mu), vec(w0), w_up, vec(a0), a_up, g_up, vec(k_k), vec(k_a), vec(r_k), tril, hsum)


def _rwkv_b_kernel(at_ref, bt_ref, kt_ref, rt_ref, v_ref, gc_ref, y_ref, s_ref):
    c = pl.program_id(1)

    @pl.when(c == 0)
    def _():
        s_ref[...] = jnp.zeros_like(s_ref)

    C = RW_CHUNK
    P = 2 * C
    lane = lax.broadcasted_iota(jnp.int32, (C, P), 1)
    lo = lane < R_HEAD_DIM
    ri = lax.broadcasted_iota(jnp.int32, (P, P), 0)
    ci = lax.broadcasted_iota(jnp.int32, (P, P), 1)
    same = (ri // C) == (ci // C)
    strict = jnp.logical_and(same, ri > ci)
    incl = jnp.logical_and(same, ri >= ci)
    eye = jnp.where(ri == ci, 1.0, 0.0).astype(F32)
    mm = _dot_bf16
    zero = jnp.zeros((), at_ref.dtype)

    def stack(x):
        return jnp.concatenate([jnp.where(lo, x, zero), jnp.where(lo, zero, x)], axis=0)

    for p in range(R_HEADS // 2):
        sl = slice(p * P, (p + 1) * P)
        a_s, b_s, k_s = stack(at_ref[:, sl]), stack(bt_ref[:, sl]), stack(kt_ref[:, sl])
        r_s, v_s = stack(rt_ref[:, sl]), stack(v_ref[:, sl])
        bk_s = jnp.concatenate([b_s, k_s], axis=0)
        g = mm(jnp.concatenate([a_s, r_s], axis=0), bk_s, _NT)
        l_ab = jnp.where(strict, g[0:P, 0:P], 0.0)
        l_ak = jnp.where(strict, g[0:P, P:2 * P], 0.0)
        m_r = jnp.where(jnp.concatenate([incl, incl], axis=1), g[P:2 * P, :], 0.0)
        t_inv = eye + l_ab
        pw = l_ab
        for _ in range(C.bit_length() - 2):
            pw = mm(pw, pw)
            t_inv = t_inv + mm(t_inv, pw)
        s0 = s_ref[p]
        z = mm(a_s, s0, _NT) + mm(l_ak, v_s)
        uv = jnp.concatenate([mm(t_inv, z).astype(at_ref.dtype), v_s], axis=0)
        y_s = mm(r_s, s0, _NT) + mm(m_r, uv)
        y_ref[:, sl] = y_s[0:C, :] + y_s[C:P, :]
        s_ref[p] = (s0 + mm(uv, bk_s, _TN)) * gc_ref[0, :, sl]


def _rwkv_b_call(at, bt, kt, rt, v, gc, B, lp):
    T, W = at.shape
    nc = lp // RW_CHUNK
    rowspec = pl.BlockSpec((RW_CHUNK, W), lambda b, c: (b * nc + c, 0))
    return pl.pallas_call(
        _rwkv_b_kernel,
        grid=(B, nc),
        in_specs=[rowspec, rowspec, rowspec, rowspec, rowspec,
                  pl.BlockSpec((1, 1, W), lambda b, c: (b * nc + c, 0, 0))],
        out_specs=rowspec,
        out_shape=jax.ShapeDtypeStruct((T, W), F32),
        scratch_shapes=[pltpu.VMEM((R_HEADS // 2, 2 * RW_CHUNK, 2 * R_HEAD_DIM), F32)],
        compiler_params=_params(("parallel", "arbitrary")),
        name="rwkv_b",
    )(at, bt, kt, rt, v, gc)


def _merge_kernel(h_ref, oa_ref, y_ref, bonus_ref, g_ref, ug_ref, bg_ref, gng_ref, gnb_ref, hmean_ref,
                  wa_ref, wb_ref, wo_ref, lng_ref, lnb_ref, o_ref, *, alpha):
    y = y_ref[...]
    hmean = hmean_ref[...]
    m = _dot_split(y, hmean)
    yc = y - m
    var = _dot_split(yc * yc, hmean)
    yn = yc * lax.rsqrt(var + GN_EPS) * gng_ref[...] + gnb_ref[...]
    ob = (yn + bonus_ref[...]) * g_ref[...]
    gates = jax.nn.sigmoid(ug_ref[...] + bg_ref[...])
    merged = (gates[:, 0:D_MODEL] * _dot_bf16(oa_ref[...], wa_ref[...])
              + gates[:, D_MODEL:] * _dot_bf16(ob, wb_ref[...]))
    mix = _dot_bf16(merged, wo_ref[...])
    o_ref[...] = _layer_norm(alpha * h_ref[...] + mix, lng_ref[...], lnb_ref[...])


def _merge_call(h, oa, y, bonus, g, ug, b_gate, gn_g, gn_b, wa, wb, wo, ln_g, ln_b, alpha):
    T, D = h.shape
    W = R_WIDTH
    hr = lax.broadcasted_iota(jnp.int32, (W, W), 0) // R_HEAD_DIM
    hc = lax.broadcasted_iota(jnp.int32, (W, W), 1) // R_HEAD_DIM
    hmean = jnp.where(hr == hc, 1.0 / R_HEAD_DIM, 0.0).astype(BF16)
    vec = lambda x: x.reshape(1, -1)
    row = lambda n: pl.BlockSpec((ROW_TILE, n), lambda i: (i, 0))
    const = lambda a, b: pl.BlockSpec((a, b), lambda i: (0, 0))
    return pl.pallas_call(
        functools.partial(_merge_kernel, alpha=alpha),
        grid=(T // ROW_TILE,),
        in_specs=[row(D), row(A_Q), row(W), row(W), row(W), row(GATE_COLS), const(1, GATE_COLS),
                  const(1, W), const(1, W), const(W, W), const(A_Q, D), const(W, D), const(D, D),
                  const(1, D), const(1, D)],
        out_specs=row(D),
        out_shape=jax.ShapeDtypeStruct((T, D), F32),
        compiler_params=_params(("parallel",)),
        name="merge",
    )(h, oa, y, bonus, g, ug, vec(b_gate), vec(gn_g), vec(gn_b), hmean,
      wa.astype(BF16), wb.astype(BF16), wo.astype(BF16), vec(ln_g), vec(ln_b))


MOE_TILES = (1536, 1024, 768, 512, 256)


def _route(logits_t, bias_t):
    s = jax.nn.sigmoid(logits_t)
    sel = s + bias_t
    row = lambda a, e: a[e:e + 1, :]
    gscore = []
    for gi in range(N_GROUPS):
        a, b, c, d = (row(sel, gi * EXPERTS_PER_GROUP + j) for j in range(EXPERTS_PER_GROUP))
        hi1, lo1 = jnp.maximum(a, b), jnp.minimum(a, b)
        hi2, lo2 = jnp.maximum(c, d), jnp.minimum(c, d)
        top1 = jnp.maximum(hi1, hi2)
        top2 = jnp.maximum(jnp.minimum(hi1, hi2), jnp.maximum(lo1, lo2))
        gscore.append(top1 + top2)
    chosen = []
    for gi in range(N_GROUPS):
        ok = None
        for gj in range(N_GROUPS):
            if gj == gi:
                continue
            t = gscore[gi] > gscore[gj] if gj < gi else gscore[gi] >= gscore[gj]
            ok = t if ok is None else jnp.logical_and(ok, t)
        chosen.append(ok)
    picked = []
    for e in range(N_EXPERTS):
        gi = e // EXPERTS_PER_GROUP
        rank = jnp.zeros_like(row(sel, e))
        for e2 in range(gi * EXPERTS_PER_GROUP, (gi + 1) * EXPERTS_PER_GROUP):
            if e2 == e:
                continue
            ahead = row(sel, e2) > row(sel, e) if e2 > e else row(sel, e2) >= row(sel, e)
            rank = rank + jnp.where(ahead, 1.0, 0.0)
        take = jnp.logical_and(chosen[gi], rank < 2.0)
        picked.append(jnp.where(take, row(s, e), 0.0))
    total = picked[0]
    for e in range(1, N_EXPERTS):
        total = total + picked[e]
    return jnp.concatenate(picked, axis=0) / total


def _moe_kernel(h_ref, wr_ref, br_ref, w1_ref, w3_ref, w2_ref, lng_ref, lnb_ref, o_ref,
                xb_ref, comb_ref, acc_ref, *, alpha):
    e = pl.program_id(1)

    @pl.when(e == 0)
    def _():
        x = h_ref[...]
        xb_ref[...] = x.astype(BF16)
        logits_t = _dot(wr_ref[...], x, _NT, precision=lax.Precision.HIGHEST)
        comb_t = _route(logits_t, br_ref[...])
        pad = jnp.zeros((LANES - N_EXPERTS, comb_t.shape[1]), F32)
        comb_ref[...] = jnp.transpose(jnp.concatenate([comb_t, pad], axis=0))
        acc_ref[...] = jnp.zeros_like(acc_ref)

    xb = xb_ref[...]
    h1 = _dot(xb, w1_ref[0])
    h3 = _dot(xb, w3_ref[0])
    act = (h1 * jax.nn.sigmoid(h1)) * h3
    he = _dot(act.astype(BF16), w2_ref[0])
    lane = lax.broadcasted_iota(jnp.int32, comb_ref.shape, 1)
    ce = jnp.sum(jnp.where(lane == e, comb_ref[...], 0.0), axis=-1, keepdims=True)
    acc_ref[...] += ce * he

    @pl.when(e == pl.num_programs(1) - 1)
    def _():
        o_ref[...] = _layer_norm(alpha * h_ref[...] + acc_ref[...], lng_ref[...], lnb_ref[...])


def _moe_call(h, w_router_t, b_router, w1, w3, w2, ln_g, ln_b, alpha, tile):
    T, D = h.shape
    vec = lambda x: x.reshape(1, -1)
    return pl.pallas_call(
        functools.partial(_moe_kernel, alpha=alpha),
        grid=(T // tile, N_EXPERTS),
        in_specs=[pl.BlockSpec((tile, D), lambda i, e: (i, 0)),
                  pl.BlockSpec((N_EXPERTS, D), lambda i, e: (0, 0)),
                  pl.BlockSpec((N_EXPERTS, 1), lambda i, e: (0, 0)),
                  pl.BlockSpec((1, D, D_EXPERT), lambda i, e: (e, 0, 0)),
                  pl.BlockSpec((1, D, D_EXPERT), lambda i, e: (e, 0, 0)),
                  pl.BlockSpec((1, D_EXPERT, D), lambda i, e: (e, 0, 0)),
                  pl.BlockSpec((1, D), lambda i, e: (0, 0)),
                  pl.BlockSpec((1, D), lambda i, e: (0, 0))],
        out_specs=pl.BlockSpec((tile, D), lambda i, e: (i, 0)),
        out_shape=jax.ShapeDtypeStruct((T, D), F32),
        scratch_shapes=[pltpu.VMEM((tile, D), BF16), pltpu.VMEM((tile, LANES), F32),
                        pltpu.VMEM((tile, D), F32)],
        compiler_params=_params(("parallel", "arbitrary")),
        name="moe",
    )(h, w_router_t, b_router.reshape(N_EXPERTS, 1), w1, w3, w2, vec(ln_g), vec(ln_b))


def _split_w_in(w):
    o = 0
    q = w[:, o:o + A_Q]; o += A_Q
    k = w[:, o:o + A_KV]; o += A_KV
    v = w[:, o:o + A_KV]; o += A_KV
    qi = w[:, o:o + IDX_Q]; o += IDX_Q
    kiw = w[:, o:o + IDX_DIM + IDX_HEADS]; o += IDX_DIM + IDX_HEADS
    pad = jnp.zeros((w.shape[0], LANES - IDX_DIM - IDX_HEADS), w.dtype)
    wa = jnp.concatenate([q, qi, k, v, kiw, pad], axis=1)
    wr = w[:, o:o + R_COLS]; o += R_COLS
    wg = w[:, o:o + GATE_COLS]
    return wa.astype(BF16), wr.astype(BF16), wg.astype(BF16)


def _padded_len(L):
    return -(-L // Q_BLOCK) * Q_BLOCK


def kernel(x, meta_tokens, ln_in_g, ln_in_b, w_in, b_gate, rwkv_mu, rwkv_w0, rwkv_w_up, rwkv_a0, rwkv_a_up,
           rwkv_g_up, rwkv_k_k, rwkv_k_a, rwkv_r_k, rwkv_gn_g, rwkv_gn_b, w_branch_a, w_branch_b, w_out,
           ln1_g, ln1_b, ln2_g, ln2_b, w_router, b_router, w_exp1, w_exp3, w_exp2):
    B, S, D = x.shape
    depth = w_in.shape[0]
    alpha = (2 * depth) ** 0.25
    topk = min(TOPK_MAX, S // 4)
    L = N_META + S
    lp = _padded_len(L)
    T = B * lp
    assert T % ROW_TILE == 0
    moe_tile = next(t for t in MOE_TILES if T % t == 0)
    meta = jnp.broadcast_to(meta_tokens[None].astype(x.dtype), (B, N_META, D))
    hin = jnp.concatenate([meta, x, jnp.zeros((B, lp - L, D), x.dtype)], axis=1).reshape(T, D)
    h = _ln_call(hin, ln_in_g, ln_in_b)
    w_router_t = jnp.transpose(w_router)
    for l in range(depth):
        wa, wr, wg = _split_w_in(w_in[l])
        ua, ur, ug = _proj_call(h, wa, wr, wg)
        oa = _dsa_call(ua, B, lp, topk)
        at, bt, kt, rt, v, gc, bonus, g = _rwkv_a_call(
            ur, B, lp, rwkv_mu[l], rwkv_w0[l], rwkv_w_up[l], rwkv_a0[l], rwkv_a_up[l], rwkv_g_up[l],
            rwkv_k_k[l], rwkv_k_a[l], rwkv_r_k[l])
        y = _rwkv_b_call(at, bt, kt, rt, v, gc, B, lp)
        h = _merge_call(h, oa, y, bonus, g, ug, b_gate[l], rwkv_gn_g[l], rwkv_gn_b[l],
                        w_branch_a[l], w_branch_b[l], w_out[l], ln1_g[l], ln1_b[l], alpha)
        h = _moe_call(h, w_router_t, b_router, w_exp1[l].astype(BF16), w_exp3[l].astype(BF16),
                      w_exp2[l].astype(BF16), ln2_g[l], ln2_b[l], alpha, moe_tile)
    return h.reshape(B, lp, D)[:, N_META:L]
```

```python
import functools

import jax
import jax.numpy as jnp
from jax import lax
from jax.experimental import pallas as pl
from jax.experimental.pallas import tpu as pltpu

D_MODEL = 1024
CHUNK = 64
N_META = 16
Q_BLOCK = 128
A_HEADS = 8
A_KV_HEADS = 2
A_HEAD_DIM = 64
IDX_HEADS = 8
IDX_DIM = 32
TOPK_MAX = 256
A_Q = A_HEADS * A_HEAD_DIM
A_KV = A_KV_HEADS * A_HEAD_DIM
IDX_Q = IDX_HEADS * IDX_DIM
R_HEADS = 8
R_HEAD_DIM = 64
R_WIDTH = R_HEADS * R_HEAD_DIM
W_LORA = 64
A_LORA = 64
G_LORA = 128
R_COLS = 3 * R_WIDTH + W_LORA + A_LORA + G_LORA
GN_EPS = 64e-5
GATE_COLS = 2 * D_MODEL
N_EXPERTS = 16
N_GROUPS = 4
EXPERTS_PER_GROUP = N_EXPERTS // N_GROUPS
D_EXPERT = 512
LN_EPS = 1e-5

LANES = 128
UA_COLS = A_Q + IDX_Q + 2 * A_KV + LANES
KIW_BLOCK = (A_Q + IDX_Q + 2 * A_KV) // LANES
ROW_TILE = 256
RWKV_A_TILE = 128
RW_CHUNK = 64
VMEM_LIMIT = 56 * 1024 * 1024

F32 = jnp.float32
BF16 = jnp.bfloat16
INT_MIN = -2147483648


def _dot(a, b, dims=(((1,), (0,)), ((), ())), precision=None):
    return lax.dot_general(a, b, dims, precision=precision, preferred_element_type=F32)


def _dot_bf16(a, b, dims=(((1,), (0,)), ((), ()))):
    return _dot(a.astype(BF16), b.astype(BF16), dims)


_NT = (((1,), (1,)), ((), ()))
_TN = (((0,), (0,)), ((), ()))


def _dot_split(x, m_bf16):
    hi = x.astype(BF16)
    lo = (x - hi.astype(F32)).astype(BF16)
    return _dot(hi, m_bf16) + _dot(lo, m_bf16)


def _layer_norm(x, g, b):
    mu = jnp.mean(x, axis=-1, keepdims=True)
    xc = x - mu
    var = jnp.mean(xc * xc, axis=-1, keepdims=True)
    return xc * lax.rsqrt(var + LN_EPS) * g + b


def _params(sem):
    return pltpu.CompilerParams(dimension_semantics=sem, vmem_limit_bytes=VMEM_LIMIT)


def _ln_kernel(x_ref, g_ref, b_ref, o_ref):
    o_ref[...] = _layer_norm(x_ref[...], g_ref[...], b_ref[...])


def _ln_call(x, g, b):
    T, D = x.shape
    return pl.pallas_call(
        _ln_kernel,
        grid=(T // ROW_TILE,),
        in_specs=[pl.BlockSpec((ROW_TILE, D), lambda i: (i, 0)),
                  pl.BlockSpec((1, D), lambda i: (0, 0)),
                  pl.BlockSpec((1, D), lambda i: (0, 0))],
        out_specs=pl.BlockSpec((ROW_TILE, D), lambda i: (i, 0)),
        out_shape=jax.ShapeDtypeStruct((T, D), F32),
        compiler_params=_params(("parallel",)),
        name="ln_in",
    )(x, g.reshape(1, D), b.reshape(1, D))


def _proj_kernel(h_ref, wa_ref, wr_ref, wg_ref, ua_ref, kiw_ref, ur_ref, ug_ref):
    hb = h_ref[...].astype(BF16)
    ua = _dot(hb, wa_ref[...])
    ua_ref[...] = ua.astype(BF16)
    kiw_ref[...] = ua[:, UA_COLS - LANES:]
    ur_ref[...] = _dot(hb, wr_ref[...])
    ug_ref[...] = _dot(hb, wg_ref[...])


def _proj_call(h, wa, wr, wg):
    T, D = h.shape
    row = lambda n: pl.BlockSpec((ROW_TILE, n), lambda i: (i, 0))
    full = lambda n: pl.BlockSpec((D, n), lambda i: (0, 0))
    return pl.pallas_call(
        _proj_kernel,
        grid=(T // ROW_TILE,),
        in_specs=[row(D), full(UA_COLS), full(R_COLS), full(GATE_COLS)],
        out_specs=[row(UA_COLS), row(LANES), row(R_COLS), row(GATE_COLS)],
        out_shape=[jax.ShapeDtypeStruct((T, UA_COLS), BF16),
                   jax.ShapeDtypeStruct((T, LANES), F32),
                   jax.ShapeDtypeStruct((T, R_COLS), F32),
                   jax.ShapeDtypeStruct((T, GATE_COLS), F32)],
        compiler_params=_params(("parallel",)),
        name="proj",
    )(h, wa, wr, wg)


assert CHUNK & (CHUNK - 1) == 0
CHUNK_SHIFT = CHUNK.bit_length() - 1
BISECT_STEPS = 24
BISECT_ROW_GROUPS = 2
ATT_HEADS_PER_TRIP = 2
REFINE_CAP = 256
KEY_EXTENT_FRACTIONS = (0.18, 0.36, 0.64, 1.0)


def _chunk_id(p):
    return jnp.where(p < N_META, 0, 1 + ((p - N_META) >> CHUNK_SHIFT))


def _count(mask):
    return jnp.sum(jnp.where(mask, 1.0, 0.0), axis=-1, keepdims=True)


def _any(mask):
    return jnp.max(jnp.where(mask, 1.0, 0.0)) > 0.5


NO_INDEX_LIMIT = 2 ** 30


def _bisect_step(sm, kf, st):
    lo, hi, clo, chi = st
    mid = 0.5 * (lo + hi)
    cnt = _count(sm >= mid)
    ge = cnt >= kf
    return (jnp.where(ge, mid, lo), jnp.where(ge, hi, mid), jnp.where(ge, cnt, clo), jnp.where(ge, chi, cnt))


def _select_fixed(sm, n_adm, topk):
    kf = float(topk)
    mx = jnp.max(sm, axis=-1, keepdims=True)
    mn = jnp.min(jnp.where(sm == -jnp.inf, jnp.inf, sm), axis=-1, keepdims=True)
    hi0 = jnp.where(mx >= 0.0, 2.0 * mx + 1.0, 0.5 * mx + 1.0)
    st0 = (mn, hi0, n_adm, jnp.zeros_like(mn))
    rows = sm.shape[0] // BISECT_ROW_GROUPS
    parts = [slice(g * rows, (g + 1) * rows) for g in range(BISECT_ROW_GROUPS)]

    def body(_, sts):
        return tuple(_bisect_step(sm[p], kf, st) for p, st in zip(parts, sts))

    sts = lax.fori_loop(0, BISECT_STEPS, body, tuple(tuple(x[p] for x in st0) for p in parts), unroll=2)
    return tuple(jnp.concatenate([st[c] for st in sts], axis=0) for c in range(4))


def _select_refine(sm, kpos, topk, st):
    kf = float(topk)

    def open_rows(lo, hi, clo):
        band = jnp.logical_and(sm >= lo, sm < hi)
        bmax = jnp.max(jnp.where(band, sm, -jnp.inf), axis=-1, keepdims=True)
        bmin = jnp.min(jnp.where(band, sm, jnp.inf), axis=-1, keepdims=True)
        return _any(jnp.logical_and(clo > kf, bmax != bmin)).astype(jnp.int32)

    def body(c):
        lo, hi, clo, chi = _bisect_step(sm, kf, c[0:4])
        return lo, hi, clo, chi, open_rows(lo, hi, clo), c[5] + 1

    lo, hi, clo, chi = st
    lo, hi, clo, chi, _, _ = lax.while_loop(
        lambda c: jnp.logical_and(c[4] > 0, c[5] < REFINE_CAP), body,
        (lo, hi, clo, chi, open_rows(lo, hi, clo), jnp.int32(0)))
    need = kf - chi
    band = jnp.logical_and(sm >= lo, sm < hi)
    nbits = int(sm.shape[1] - 1).bit_length()

    def idx_step(it, j):
        c = j | (jnp.int32(1) << (nbits - 1 - it))
        cnt = _count(jnp.logical_and(band, kpos < c))
        return jnp.where(cnt <= need, c, j)

    j = lax.fori_loop(0, nbits, idx_step, jnp.zeros(lo.shape, jnp.int32))
    return lo, hi, jnp.where(clo > kf, j, jnp.int32(NO_INDEX_LIMIT))


ST_LO, ST_HI, ST_CLO, ST_CHI = range(4)


def _dsa_scores(wi_ref, qih_ref, kih_ref, sm_ref, st_ref, *, i, topk, nkeys):
    lane = lax.broadcasted_iota(jnp.int32, (Q_BLOCK, LANES), 1)
    sm_ref[:, 0:nkeys] = jnp.zeros((Q_BLOCK, nkeys), F32)

    def idx_head(h, carry):
        s = _dot(qih_ref[h], kih_ref[0:nkeys, :], _NT)
        w = jnp.sum(jnp.where(lane == IDX_DIM + h, wi_ref[...], 0.0), axis=-1, keepdims=True)
        sm_ref[:, 0:nkeys] += w * jnp.maximum(s, 0.0)
        return carry

    lax.fori_loop(0, IDX_HEADS, idx_head, 0)

    qpos = i * Q_BLOCK + lax.broadcasted_iota(jnp.int32, (Q_BLOCK, 1), 0)
    kpos = lax.broadcasted_iota(jnp.int32, (1, nkeys), 1)
    adm = _chunk_id(kpos) <= _chunk_id(qpos)
    sm = jnp.where(adm, sm_ref[:, 0:nkeys], -jnp.inf)
    sm_ref[:, 0:nkeys] = sm
    st = _select_fixed(sm, _count(adm), topk)
    for c, val in enumerate(st):
        st_ref[:, c:c + 1] = val


def _dsa_attend(o_ref, qh_ref, kh_ref, vh_ref, sm_ref, st_ref, j_ref, oh_ref, *, nkeys):
    kpos = lax.broadcasted_iota(jnp.int32, (1, nkeys), 1)
    sm = sm_ref[:, 0:nkeys]
    lo, hi = st_ref[:, ST_LO:ST_LO + 1], st_ref[:, ST_HI:ST_HI + 1]
    sel = jnp.logical_or(sm >= hi, jnp.logical_and(sm >= lo, kpos < j_ref[:, 0:1]))
    sm_ref[:, 0:nkeys] = jnp.where(sel, 0.0, -jnp.inf)
    group = A_HEADS // A_KV_HEADS

    def att_heads(t, carry):
        hs = [t * ATT_HEADS_PER_TRIP + d for d in range(ATT_HEADS_PER_TRIP)]
        ns = [h // group for h in hs]
        s = [_dot(qh_ref[h], kh_ref[n, 0:nkeys, :], _NT) + sm_ref[:, 0:nkeys] for h, n in zip(hs, ns)]
        m = [jnp.max(x, axis=-1, keepdims=True) for x in s]
        p = [jnp.exp(x - mx) for x, mx in zip(s, m)]
        l = [jnp.sum(x, axis=-1, keepdims=True) for x in p]
        o = [_dot(x.astype(BF16), vh_ref[n, 0:nkeys, :]) / lx for x, n, lx in zip(p, ns, l)]
        oh_ref[pl.ds(hs[0], ATT_HEADS_PER_TRIP)] = jnp.stack(o)
        return carry

    lax.fori_loop(0, A_HEADS // ATT_HEADS_PER_TRIP, att_heads, 0)
    o_ref[...] = jnp.concatenate([oh_ref[h] for h in range(A_HEADS)], axis=-1)


def _key_extents(nq):
    return sorted({min(nq, max(1, round(nq * f))) for f in KEY_EXTENT_FRACTIONS} | {nq})


def _dsa_kernel(q_ref, qi_ref, wi_ref, k_ref, v_ref, ki_ref, o_ref,
                qh_ref, qih_ref, kh_ref, vh_ref, kih_ref, sm_ref, st_ref, j_ref, oh_ref, *, topk, nq):
    i = pl.program_id(1)
    lp = nq * Q_BLOCK

    @pl.when(i == 0)
    def _():
        for n in range(A_KV_HEADS):
            kh_ref[n] = k_ref[:, n * A_HEAD_DIM:(n + 1) * A_HEAD_DIM]
            vh_ref[n] = v_ref[:, n * A_HEAD_DIM:(n + 1) * A_HEAD_DIM]
        kih_ref[...] = ki_ref[:, 0:IDX_DIM]

    scale = jnp.asarray(A_HEAD_DIM ** -0.5, BF16)
    for h in range(A_HEADS):
        qh_ref[h] = q_ref[:, h * A_HEAD_DIM:(h + 1) * A_HEAD_DIM] * scale
    for h in range(IDX_HEADS):
        qih_ref[h] = qi_ref[:, h * IDX_DIM:(h + 1) * IDX_DIM]

    last_chunk = 1 + ((i + 1) * Q_BLOCK - 1 - N_META) // CHUNK
    keys_end = N_META + CHUNK * last_chunk
    need = jnp.minimum((keys_end + Q_BLOCK - 1) // Q_BLOCK, nq)
    extents = _key_extents(nq)
    lows = [0] + extents[:-1]
    in_class = [jnp.logical_and(need > a, need <= e) for a, e in zip(lows, extents)]

    for e, on in zip(extents, in_class):
        @pl.when(on)
        def _(e=e):
            _dsa_scores(wi_ref, qih_ref, kih_ref, sm_ref, st_ref, i=i, topk=topk, nkeys=e * Q_BLOCK)

    j_ref[...] = jnp.full(j_ref.shape, NO_INDEX_LIMIT, jnp.int32)

    @pl.when(_any(st_ref[:, ST_CLO:ST_CLO + 1] > float(topk)))
    def _():
        nkeys = Q_BLOCK * sum(jnp.where(on, e, 0) for e, on in zip(extents, in_class))
        kpos = lax.broadcasted_iota(jnp.int32, (1, lp), 1)
        sm = jnp.where(kpos < nkeys, sm_ref[...], -jnp.inf)
        st = tuple(st_ref[:, c:c + 1] for c in range(4))
        lo, hi, j = _select_refine(sm, kpos, topk, st)
        st_ref[:, ST_LO:ST_LO + 1] = lo
        st_ref[:, ST_HI:ST_HI + 1] = hi
        j_ref[:, 0:1] = j

    for e, on in zip(extents, in_class):
        @pl.when(on)
        def _(e=e):
            _dsa_attend(o_ref, qh_ref, kh_ref, vh_ref, sm_ref, st_ref, j_ref, oh_ref, nkeys=e * Q_BLOCK)


def _dsa_call(ua, kiw, B, lp, topk):
    nq = lp // Q_BLOCK
    kern = functools.partial(_dsa_kernel, topk=topk, nq=nq)
    qrow = lambda b, i: b * nq + i
    return pl.pallas_call(
        kern,
        grid=(B, nq),
        in_specs=[pl.BlockSpec((Q_BLOCK, A_Q), lambda b, i: (qrow(b, i), 0)),
                  pl.BlockSpec((Q_BLOCK, IDX_Q), lambda b, i: (qrow(b, i), A_Q // IDX_Q)),
                  pl.BlockSpec((Q_BLOCK, LANES), lambda b, i: (qrow(b, i), 0)),
                  pl.BlockSpec((lp, A_KV), lambda b, i: (b, (A_Q + IDX_Q) // A_KV)),
                  pl.BlockSpec((lp, A_KV), lambda b, i: (b, (A_Q + IDX_Q) // A_KV + 1)),
                  pl.BlockSpec((lp, LANES), lambda b, i: (b, KIW_BLOCK))],
        out_specs=pl.BlockSpec((Q_BLOCK, A_Q), lambda b, i: (qrow(b, i), 0)),
        out_shape=jax.ShapeDtypeStruct((B * lp, A_Q), F32),
        scratch_shapes=[pltpu.VMEM((A_HEADS, Q_BLOCK, A_HEAD_DIM), BF16),
                        pltpu.VMEM((IDX_HEADS, Q_BLOCK, IDX_DIM), BF16),
                        pltpu.VMEM((A_KV_HEADS, lp, A_HEAD_DIM), BF16),
                        pltpu.VMEM((A_KV_HEADS, lp, A_HEAD_DIM), BF16),
                        pltpu.VMEM((lp, IDX_DIM), BF16),
                        pltpu.VMEM((Q_BLOCK, lp), F32),
                        pltpu.VMEM((Q_BLOCK, LANES), F32),
                        pltpu.VMEM((Q_BLOCK, LANES), jnp.int32),
                        pltpu.VMEM((A_HEADS, Q_BLOCK, A_HEAD_DIM), F32)],
        compiler_params=_params(("parallel", "arbitrary")),
        name="dsa",
    )(ua, ua, kiw, ua, ua, ua)


def _rwkv_a_kernel(ur_ref, mu_ref, w0_ref, wup_ref, a0_ref, aup_ref, gup_ref, kk_ref, ka_ref, rk_ref,
                   tril_ref, hsum_ref,
                   at_ref, bt_ref, kt_ref, rt_ref, v_ref, gc_ref, bonus_ref, g_ref, prev_ref):
    t = pl.program_id(1)
    tm = ur_ref.shape[0]

    @pl.when(t == 0)
    def _():
        prev_ref[...] = jnp.zeros_like(prev_ref)

    u = ur_ref[...]
    rows = lax.broadcasted_iota(jnp.int32, (tm, 1), 0)
    u_prev = jnp.where(rows == 0, prev_ref[0:1, :], pltpu.roll(u, 1, 0))
    prev_ref[0:1, :] = u[tm - 1:tm, :]
    u = u + (u_prev - u) * mu_ref[...]

    W = R_WIDTH
    r, k, v = u[:, 0:W], u[:, W:2 * W], u[:, 2 * W:3 * W]
    xw = u[:, 3 * W:3 * W + W_LORA]
    xa = u[:, 3 * W + W_LORA:3 * W + W_LORA + A_LORA]
    xg = u[:, 3 * W + W_LORA + A_LORA:]

    hp = lax.Precision.HIGHEST
    zw = w0_ref[...] + _dot(jnp.tanh(xw), wup_ref[...], precision=hp)
    logw = -(jnp.maximum(-zw, 0.0) + jnp.log1p(jnp.exp(-jnp.abs(zw)))) - 0.5
    lw = -jnp.exp(logw)
    a = jax.nn.sigmoid(a0_ref[...] + _dot(xa, aup_ref[...], precision=hp))
    g = _dot(jax.nn.sigmoid(xg), gup_ref[...], precision=hp)

    hsum = hsum_ref[...]
    kk = k * kk_ref[...]
    kk = kk * lax.rsqrt(jnp.maximum(_dot_split(kk * kk, hsum), 1e-24))
    kmod = k * (1.0 + (a - 1.0) * ka_ref[...])
    bonus_ref[...] = _dot_split(r * kmod * rk_ref[...], hsum) * v
    g_ref[...] = g
    v_ref[...] = v.astype(v_ref.dtype)

    cs = _dot(tril_ref[...], lw, precision=hp)
    e_neg = jnp.exp(-cs)
    at_ref[...] = (-kk * jnp.exp(cs - lw)).astype(at_ref.dtype)
    bt_ref[...] = (kk * a * e_neg).astype(bt_ref.dtype)
    kt_ref[...] = (kmod * e_neg).astype(kt_ref.dtype)
    rt_ref[...] = (r * jnp.exp(cs)).astype(rt_ref.dtype)
    for c in range(tm // RW_CHUNK):
        gc_ref[c] = jnp.exp(cs[(c + 1) * RW_CHUNK - 1:(c + 1) * RW_CHUNK, :])


def _rwkv_a_call(ur, B, lp, mu, w0, w_up, a0, a_up, g_up, k_k, k_a, r_k):
    T = B * lp
    tm = RWKV_A_TILE
    nt = lp // tm
    W = R_WIDTH
    ri = lax.broadcasted_iota(jnp.int32, (tm, tm), 0)
    ci = lax.broadcasted_iota(jnp.int32, (tm, tm), 1)
    tril = jnp.where(jnp.logical_and(ri >= ci, ri // RW_CHUNK == ci // RW_CHUNK), 1.0, 0.0).astype(F32)
    hr = lax.broadcasted_iota(jnp.int32, (W, W), 0) // R_HEAD_DIM
    hc = lax.broadcasted_iota(jnp.int32, (W, W), 1) // R_HEAD_DIM
    hsum = jnp.where(hr == hc, 1.0, 0.0).astype(BF16)
    vec = lambda x: x.reshape(1, -1)
    const = lambda shape: pl.BlockSpec(shape, lambda b, t: (0,) * len(shape))
    rowspec = lambda n: pl.BlockSpec((tm, n), lambda b, t: (b * nt + t, 0))
    nck = tm // RW_CHUNK
    out_rows = jax.ShapeDtypeStruct((T, W), F32)
    mxu_rows = jax.ShapeDtypeStruct((T, W), BF16)
    return pl.pallas_call(
        _rwkv_a_kernel,
        grid=(B, nt),
        in_specs=[rowspec(R_COLS), const((1, R_COLS)), const((1, W)), const((W_LORA, W)), const((1, W)),
                  const((A_LORA, W)), const((G_LORA, W)), const((1, W)), const((1, W)), const((1, W)),
                  const((tm, tm)), const((W, W))],
        out_specs=[rowspec(W), rowspec(W), rowspec(W), rowspec(W), rowspec(W),
                   pl.BlockSpec((nck, 1, W), lambda b, t: (b * nt + t, 0, 0)),
                   rowspec(W), rowspec(W)],
        out_shape=[mxu_rows, mxu_rows, mxu_rows, mxu_rows, mxu_rows,
                   jax.ShapeDtypeStruct((T // RW_CHUNK, 1, W), F32), out_rows, out_rows],
        scratch_shapes=[pltpu.VMEM((8, R_COLS), F32)],
        compiler_params=_params(("parallel", "arbitrary")),
        name="rwkv_a",
    )(ur, vec(mu), vec(w0), w_up, vec(a0), a_up, g_up, vec(k_k), vec(k_a), vec(r_k), tril, hsum)


def _rwkv_b_kernel(at_ref, bt_ref, kt_ref, rt_ref, v_ref, gc_ref, y_ref, s_ref):
    c = pl.program_id(1)

    @pl.when(c == 0)
    def _():
        s_ref[...] = jnp.zeros_like(s_ref)

    C = RW_CHUNK
    P = 2 * C
    lane = lax.broadcasted_iota(jnp.int32, (C, P), 1)
    lo = lane < R_HEAD_DIM
    ri = lax.broadcasted_iota(jnp.int32, (P, P), 0)
    ci = lax.broadcasted_iota(jnp.int32, (P, P), 1)
    same = (ri // C) == (ci // C)
    strict = jnp.logical_and(same, ri > ci)
    incl = jnp.logical_and(same, ri >= ci)
    eye = jnp.where(ri == ci, 1.0, 0.0).astype(F32)
    mm = _dot_bf16
    zero = jnp.zeros((), at_ref.dtype)

    def stack(x):
        return jnp.concatenate([jnp.where(lo, x, zero), jnp.where(lo, zero, x)], axis=0)

    pairs = range(R_HEADS // 2)
    sls = [slice(p * P, (p + 1) * P) for p in pairs]
    a_s = [stack(at_ref[:, sl]) for sl in sls]
    r_s = [stack(rt_ref[:, sl]) for sl in sls]
    v_s = [stack(v_ref[:, sl]) for sl in sls]
    bk_s = [jnp.concatenate([stack(bt_ref[:, sl]), stack(kt_ref[:, sl])], axis=0) for sl in sls]
    s0 = [s_ref[p] for p in pairs]
    g = [mm(jnp.concatenate([a_s[p], r_s[p]], axis=0), bk_s[p], _NT) for p in pairs]
    l_ab = [jnp.where(strict, g[p][0:P, 0:P], 0.0) for p in pairs]
    l_ak = [jnp.where(strict, g[p][0:P, P:2 * P], 0.0) for p in pairs]
    m_r = [jnp.where(jnp.concatenate([incl, incl], axis=1), g[p][P:2 * P, :], 0.0) for p in pairs]
    t_inv = [eye + l_ab[p] for p in pairs]
    pw = l_ab
    for _ in range(C.bit_length() - 2):
        pw = [mm(pw[p], pw[p]) for p in pairs]
        t_inv = [t_inv[p] + mm(t_inv[p], pw[p]) for p in pairs]
    z = [mm(a_s[p], s0[p], _NT) + mm(l_ak[p], v_s[p]) for p in pairs]
    uv = [jnp.concatenate([mm(t_inv[p], z[p]).astype(at_ref.dtype), v_s[p]], axis=0) for p in pairs]
    y_s = [mm(r_s[p], s0[p], _NT) + mm(m_r[p], uv[p]) for p in pairs]
    s_new = [(s0[p] + mm(uv[p], bk_s[p], _TN)) * gc_ref[0, :, sls[p]] for p in pairs]
    y_ref[...] = jnp.concatenate([y[0:C, :] + y[C:P, :] for y in y_s], axis=-1)
    s_ref[...] = jnp.stack(s_new, axis=0)


def _rwkv_b_call(at, bt, kt, rt, v, gc, B, lp):
    T, W = at.shape
    nc = lp // RW_CHUNK
    rowspec = pl.BlockSpec((RW_CHUNK, W), lambda b, c: (b * nc + c, 0))
    return pl.pallas_call(
        _rwkv_b_kernel,
        grid=(B, nc),
        in_specs=[rowspec, rowspec, rowspec, rowspec, rowspec,
                  pl.BlockSpec((1, 1, W), lambda b, c: (b * nc + c, 0, 0))],
        out_specs=rowspec,
        out_shape=jax.ShapeDtypeStruct((T, W), F32),
        scratch_shapes=[pltpu.VMEM((R_HEADS // 2, 2 * RW_CHUNK, 2 * R_HEAD_DIM), F32)],
        compiler_params=_params(("parallel", "arbitrary")),
        name="rwkv_b",
    )(at, bt, kt, rt, v, gc)


def _merge_kernel(h_ref, oa_ref, y_ref, bonus_ref, g_ref, ug_ref, bg_ref, gng_ref, gnb_ref, hmean_ref,
                  wa_ref, wb_ref, wo_ref, lng_ref, lnb_ref, o_ref, *, alpha):
    y = y_ref[...]
    hmean = hmean_ref[...]
    m = _dot_split(y, hmean)
    yc = y - m
    var = _dot_split(yc * yc, hmean)
    yn = yc * lax.rsqrt(var + GN_EPS) * gng_ref[...] + gnb_ref[...]
    ob = (yn + bonus_ref[...]) * g_ref[...]
    gates = jax.nn.sigmoid(ug_ref[...] + bg_ref[...])
    merged = (gates[:, 0:D_MODEL] * _dot_bf16(oa_ref[...], wa_ref[...])
              + gates[:, D_MODEL:] * _dot_bf16(ob, wb_ref[...]))
    mix = _dot_bf16(merged, wo_ref[...])
    o_ref[...] = _layer_norm(alpha * h_ref[...] + mix, lng_ref[...], lnb_ref[...])


def _merge_call(h, oa, y, bonus, g, ug, b_gate, gn_g, gn_b, wa, wb, wo, ln_g, ln_b, alpha):
    T, D = h.shape
    W = R_WIDTH
    hr = lax.broadcasted_iota(jnp.int32, (W, W), 0) // R_HEAD_DIM
    hc = lax.broadcasted_iota(jnp.int32, (W, W), 1) // R_HEAD_DIM
    hmean = jnp.where(hr == hc, 1.0 / R_HEAD_DIM, 0.0).astype(BF16)
    vec = lambda x: x.reshape(1, -1)
    row = lambda n: pl.BlockSpec((ROW_TILE, n), lambda i: (i, 0))
    const = lambda a, b: pl.BlockSpec((a, b), lambda i: (0, 0))
    return pl.pallas_call(
        functools.partial(_merge_kernel, alpha=alpha),
        grid=(T // ROW_TILE,),
        in_specs=[row(D), row(A_Q), row(W), row(W), row(W), row(GATE_COLS), const(1, GATE_COLS),
                  const(1, W), const(1, W), const(W, W), const(A_Q, D), const(W, D), const(D, D),
                  const(1, D), const(1, D)],
        out_specs=row(D),
        out_shape=jax.ShapeDtypeStruct((T, D), F32),
        compiler_params=_params(("parallel",)),
        name="merge",
    )(h, oa, y, bonus, g, ug, vec(b_gate), vec(gn_g), vec(gn_b), hmean,
      wa.astype(BF16), wb.astype(BF16), wo.astype(BF16), vec(ln_g), vec(ln_b))


MOE_TILES = (1536, 1024, 768, 512, 256)


def _route(logits_t, bias_t):
    s = jax.nn.sigmoid(logits_t)
    sel = s + bias_t
    row = lambda a, e: a[e:e + 1, :]
    gscore = []
    for gi in range(N_GROUPS):
        a, b, c, d = (row(sel, gi * EXPERTS_PER_GROUP + j) for j in range(EXPERTS_PER_GROUP))
        hi1, lo1 = jnp.maximum(a, b), jnp.minimum(a, b)
        hi2, lo2 = jnp.maximum(c, d), jnp.minimum(c, d)
        top1 = jnp.maximum(hi1, hi2)
        top2 = jnp.maximum(jnp.minimum(hi1, hi2), jnp.maximum(lo1, lo2))
        gscore.append(top1 + top2)
    chosen = []
    for gi in range(N_GROUPS):
        ok = None
        for gj in range(N_GROUPS):
            if gj == gi:
                continue
            t = gscore[gi] > gscore[gj] if gj < gi else gscore[gi] >= gscore[gj]
            ok = t if ok is None else jnp.logical_and(ok, t)
        chosen.append(ok)
    picked = []
    for e in range(N_EXPERTS):
        gi = e // EXPERTS_PER_GROUP
        rank = jnp.zeros_like(row(sel, e))
        for e2 in range(gi * EXPERTS_PER_GROUP, (gi + 1) * EXPERTS_PER_GROUP):
            if e2 == e:
                continue
            ahead = row(sel, e2) > row(sel, e) if e2 > e else row(sel, e2) >= row(sel, e)
            rank = rank + jnp.where(ahead, 1.0, 0.0)
        take = jnp.logical_and(chosen[gi], rank < 2.0)
        picked.append(jnp.where(take, row(s, e), 0.0))
    total = picked[0]
    for e in range(1, N_EXPERTS):
        total = total + picked[e]
    return jnp.concatenate(picked, axis=0) / total


def _moe_kernel(h_ref, wr_ref, br_ref, w1_ref, w3_ref, w2_ref, lng_ref, lnb_ref, o_ref,
                xb_ref, comb_ref, acc_ref, *, alpha):
    e = pl.program_id(1)

    @pl.when(e == 0)
    def _():
        x = h_ref[...]
        xb_ref[...] = x.astype(BF16)
        logits_t = _dot(wr_ref[...], x, _NT, precision=lax.Precision.HIGHEST)
        comb_t = _route(logits_t, br_ref[...])
        pad = jnp.zeros((LANES - N_EXPERTS, comb_t.shape[1]), F32)
        comb_ref[...] = jnp.transpose(jnp.concatenate([comb_t, pad], axis=0))
        acc_ref[...] = jnp.zeros_like(acc_ref)

    xb = xb_ref[...]
    h1 = _dot(xb, w1_ref[0])
    h3 = _dot(xb, w3_ref[0])
    act = (h1 * jax.nn.sigmoid(h1)) * h3
    he = _dot(act.astype(BF16), w2_ref[0])
    lane = lax.broadcasted_iota(jnp.int32, comb_ref.shape, 1)
    ce = jnp.sum(jnp.where(lane == e, comb_ref[...], 0.0), axis=-1, keepdims=True)
    acc_ref[...] += ce * he

    @pl.when(e == pl.num_programs(1) - 1)
    def _():
        o_ref[...] = _layer_norm(alpha * h_ref[...] + acc_ref[...], lng_ref[...], lnb_ref[...])


def _moe_call(h, w_router_t, b_router, w1, w3, w2, ln_g, ln_b, alpha, tile):
    T, D = h.shape
    vec = lambda x: x.reshape(1, -1)
    return pl.pallas_call(
        functools.partial(_moe_kernel, alpha=alpha),
        grid=(T // tile, N_EXPERTS),
        in_specs=[pl.BlockSpec((tile, D), lambda i, e: (i, 0)),
                  pl.BlockSpec((N_EXPERTS, D), lambda i, e: (0, 0)),
                  pl.BlockSpec((N_EXPERTS, 1), lambda i, e: (0, 0)),
                  pl.BlockSpec((1, D, D_EXPERT), lambda i, e: (e, 0, 0)),
                  pl.BlockSpec((1, D, D_EXPERT), lambda i, e: (e, 0, 0)),
                  pl.BlockSpec((1, D_EXPERT, D), lambda i, e: (e, 0, 0)),
                  pl.BlockSpec((1, D), lambda i, e: (0, 0)),
                  pl.BlockSpec((1, D), lambda i, e: (0, 0))],
        out_specs=pl.BlockSpec((tile, D), lambda i, e: (i, 0)),
        out_shape=jax.ShapeDtypeStruct((T, D), F32),
        scratch_shapes=[pltpu.VMEM((tile, D), BF16), pltpu.VMEM((tile, LANES), F32),
                        pltpu.VMEM((tile, D), F32)],
        compiler_params=_params(("parallel", "arbitrary")),
        name="moe",
    )(h, w_router_t, b_router.reshape(N_EXPERTS, 1), w1, w3, w2, vec(ln_g), vec(ln_b))


def _split_w_in(w):
    o = 0
    q = w[:, o:o + A_Q]; o += A_Q
    k = w[:, o:o + A_KV]; o += A_KV
    v = w[:, o:o + A_KV]; o += A_KV
    qi = w[:, o:o + IDX_Q]; o += IDX_Q
    kiw = w[:, o:o + IDX_DIM + IDX_HEADS]; o += IDX_DIM + IDX_HEADS
    pad = jnp.zeros((w.shape[0], LANES - IDX_DIM - IDX_HEADS), w.dtype)
    wa = jnp.concatenate([q, qi, k, v, kiw, pad], axis=1)
    wr = w[:, o:o + R_COLS]; o += R_COLS
    wg = w[:, o:o + GATE_COLS]
    return wa.astype(BF16), wr.astype(BF16), wg.astype(BF16)


def _padded_len(L):
    return -(-L // Q_BLOCK) * Q_BLOCK


def kernel(x, meta_tokens, ln_in_g, ln_in_b, w_in, b_gate, rwkv_mu, rwkv_w0, rwkv_w_up, rwkv_a0, rwkv_a_up,
           rwkv_g_up, rwkv_k_k, rwkv_k_a, rwkv_r_k, rwkv_gn_g, rwkv_gn_b, w_branch_a, w_branch_b, w_out,
           ln1_g, ln1_b, ln2_g, ln2_b, w_router, b_router, w_exp1, w_exp3, w_exp2):
    B, S, D = x.shape
    depth = w_in.shape[0]
    alpha = (2 * depth) ** 0.25
    topk = min(TOPK_MAX, S // 4)
    L = N_META + S
    lp = _padded_len(L)
    T = B * lp
    assert T % ROW_TILE == 0
    moe_tile = next(t for t in MOE_TILES if T % t == 0)
    meta = jnp.broadcast_to(meta_tokens[None].astype(x.dtype), (B, N_META, D))
    hin = jnp.concatenate([meta, x, jnp.zeros((B, lp - L, D), x.dtype)], axis=1).reshape(T, D)
    h = _ln_call(hin, ln_in_g, ln_in_b)
    w_router_t = jnp.transpose(w_router)
    for l in range(depth):
        wa, wr, wg = _split_w_in(w_in[l])
        ua, kiw, ur, ug = _proj_call(h, wa, wr, wg)
        oa = _dsa_call(ua, kiw, B, lp, topk)
        at, bt, kt, rt, v, gc, bonus, g = _rwkv_a_call(
            ur, B, lp, rwkv_mu[l], rwkv_w0[l], rwkv_w_up[l], rwkv_a0[l], rwkv_a_up[l], rwkv_g_up[l],
            rwkv_k_k[l], rwkv_k_a[l], rwkv_r_k[l])
        y = _rwkv_b_call(at, bt, kt, rt, v, gc, B, lp)
        h = _merge_call(h, oa, y, bonus, g, ug, b_gate[l], rwkv_gn_g[l], rwkv_gn_b[l],
                        w_branch_a[l], w_branch_b[l], w_out[l], ln1_g[l], ln1_b[l], alpha)
        h = _moe_call(h, w_router_t, b_router, w_exp1[l].astype(BF16), w_exp3[l].astype(BF16),
                      w_exp2[l].astype(BF16), ln2_g[l], ln2_b[l], alpha, moe_tile)
    return h.reshape(B, lp, D)[:, N_META:L]
```

```python
import functools

import jax
import jax.numpy as jnp
from jax import lax
from jax.experimental import pallas as pl
from jax.experimental.pallas import tpu as pltpu

D_MODEL = 1024
CHUNK = 64
N_META = 16
Q_BLOCK = 128
A_HEADS = 8
A_KV_HEADS = 2
A_HEAD_DIM = 64
IDX_HEADS = 8
IDX_DIM = 32
TOPK_MAX = 256
A_Q = A_HEADS * A_HEAD_DIM
A_KV = A_KV_HEADS * A_HEAD_DIM
IDX_Q = IDX_HEADS * IDX_DIM
R_HEADS = 8
R_HEAD_DIM = 64
R_WIDTH = R_HEADS * R_HEAD_DIM
W_LORA = 64
A_LORA = 64
G_LORA = 128
R_COLS = 3 * R_WIDTH + W_LORA + A_LORA + G_LORA
GN_EPS = 64e-5
GATE_COLS = 2 * D_MODEL
N_EXPERTS = 16
N_GROUPS = 4
EXPERTS_PER_GROUP = N_EXPERTS // N_GROUPS
D_EXPERT = 512
LN_EPS = 1e-5

LANES = 128
UA_COLS = A_Q + IDX_Q + 2 * A_KV + LANES
KIW_BLOCK = (A_Q + IDX_Q + 2 * A_KV) // LANES
ROW_TILE = 256
RWKV_A_TILE = 128
RW_CHUNK = 64
VMEM_LIMIT = 56 * 1024 * 1024

F32 = jnp.float32
BF16 = jnp.bfloat16
INT_MIN = -2147483648


def _dot(a, b, dims=(((1,), (0,)), ((), ())), precision=None):
    return lax.dot_general(a, b, dims, precision=precision, preferred_element_type=F32)


def _dot_bf16(a, b, dims=(((1,), (0,)), ((), ()))):
    return _dot(a.astype(BF16), b.astype(BF16), dims)


_NT = (((1,), (1,)), ((), ()))
_TN = (((0,), (0,)), ((), ()))


def _dot_split(x, m_bf16):
    hi = x.astype(BF16)
    lo = (x - hi.astype(F32)).astype(BF16)
    return _dot(hi, m_bf16) + _dot(lo, m_bf16)


def _layer_norm(x, g, b):
    mu = jnp.mean(x, axis=-1, keepdims=True)
    xc = x - mu
    var = jnp.mean(xc * xc, axis=-1, keepdims=True)
    return xc * lax.rsqrt(var + LN_EPS) * g + b


def _params(sem):
    return pltpu.CompilerParams(dimension_semantics=sem, vmem_limit_bytes=VMEM_LIMIT)


def _ln_kernel(x_ref, g_ref, b_ref, o_ref):
    o_ref[...] = _layer_norm(x_ref[...], g_ref[...], b_ref[...])


def _ln_call(x, g, b):
    T, D = x.shape
    return pl.pallas_call(
        _ln_kernel,
        grid=(T // ROW_TILE,),
        in_specs=[pl.BlockSpec((ROW_TILE, D), lambda i: (i, 0)),
                  pl.BlockSpec((1, D), lambda i: (0, 0)),
                  pl.BlockSpec((1, D), lambda i: (0, 0))],
        out_specs=pl.BlockSpec((ROW_TILE, D), lambda i: (i, 0)),
        out_shape=jax.ShapeDtypeStruct((T, D), F32),
        compiler_params=_params(("parallel",)),
        name="ln_in",
    )(x, g.reshape(1, D), b.reshape(1, D))


def _proj_kernel(h_ref, wa_ref, wr_ref, wg_ref, ua_ref, kiw_ref, ur_ref, ug_ref):
    hb = h_ref[...].astype(BF16)
    ua = _dot(hb, wa_ref[...])
    ua_ref[...] = ua.astype(BF16)
    kiw_ref[...] = ua[:, UA_COLS - LANES:]
    ur_ref[...] = _dot(hb, wr_ref[...])
    ug_ref[...] = _dot(hb, wg_ref[...])


def _proj_call(h, wa, wr, wg):
    T, D = h.shape
    row = lambda n: pl.BlockSpec((ROW_TILE, n), lambda i: (i, 0))
    full = lambda n: pl.BlockSpec((D, n), lambda i: (0, 0))
    return pl.pallas_call(
        _proj_kernel,
        grid=(T // ROW_TILE,),
        in_specs=[row(D), full(UA_COLS), full(R_COLS), full(GATE_COLS)],
        out_specs=[row(UA_COLS), row(LANES), row(R_COLS), row(GATE_COLS)],
        out_shape=[jax.ShapeDtypeStruct((T, UA_COLS), BF16),
                   jax.ShapeDtypeStruct((T, LANES), F32),
                   jax.ShapeDtypeStruct((T, R_COLS), F32),
                   jax.ShapeDtypeStruct((T, GATE_COLS), F32)],
        compiler_params=_params(("parallel",)),
        name="proj",
    )(h, wa, wr, wg)


assert CHUNK & (CHUNK - 1) == 0
CHUNK_SHIFT = CHUNK.bit_length() - 1
BISECT_STEPS = 24
BISECT_ROW_GROUPS = 2
ATT_HEADS_PER_TRIP = 2
REFINE_CAP = 256
KEY_EXTENT_FRACTIONS = (0.18, 0.36, 0.64, 1.0)


def _chunk_id(p):
    return jnp.where(p < N_META, 0, 1 + ((p - N_META) >> CHUNK_SHIFT))


def _count(mask):
    return jnp.sum(jnp.where(mask, 1.0, 0.0), axis=-1, keepdims=True)


def _any(mask):
    return jnp.max(jnp.where(mask, 1.0, 0.0)) > 0.5


NO_INDEX_LIMIT = 2 ** 30


def _bisect_step(sm, kf, st):
    lo, hi, clo, chi = st
    mid = 0.5 * (lo + hi)
    cnt = _count(sm >= mid)
    ge = cnt >= kf
    return (jnp.where(ge, mid, lo), jnp.where(ge, hi, mid), jnp.where(ge, cnt, clo), jnp.where(ge, chi, cnt))


def _select_fixed(sm, n_adm, topk):
    kf = float(topk)
    mx = jnp.max(sm, axis=-1, keepdims=True)
    mn = jnp.min(jnp.where(sm == -jnp.inf, jnp.inf, sm), axis=-1, keepdims=True)
    hi0 = jnp.where(mx >= 0.0, 2.0 * mx + 1.0, 0.5 * mx + 1.0)
    st0 = (mn, hi0, n_adm, jnp.zeros_like(mn))
    rows = sm.shape[0] // BISECT_ROW_GROUPS
    parts = [slice(g * rows, (g + 1) * rows) for g in range(BISECT_ROW_GROUPS)]

    def body(_, sts):
        return tuple(_bisect_step(sm[p], kf, st) for p, st in zip(parts, sts))

    sts = lax.fori_loop(0, BISECT_STEPS, body, tuple(tuple(x[p] for x in st0) for p in parts), unroll=2)
    return tuple(jnp.concatenate([st[c] for st in sts], axis=0) for c in range(4))


def _select_refine(sm, kpos, topk, st):
    kf = float(topk)

    def open_rows(lo, hi, clo):
        band = jnp.logical_and(sm >= lo, sm < hi)
        bmax = jnp.max(jnp.where(band, sm, -jnp.inf), axis=-1, keepdims=True)
        bmin = jnp.min(jnp.where(band, sm, jnp.inf), axis=-1, keepdims=True)
        return _any(jnp.logical_and(clo > kf, bmax != bmin)).astype(jnp.int32)

    def body(c):
        lo, hi, clo, chi = _bisect_step(sm, kf, c[0:4])
        return lo, hi, clo, chi, open_rows(lo, hi, clo), c[5] + 1

    lo, hi, clo, chi = st
    lo, hi, clo, chi, _, _ = lax.while_loop(
        lambda c: jnp.logical_and(c[4] > 0, c[5] < REFINE_CAP), body,
        (lo, hi, clo, chi, open_rows(lo, hi, clo), jnp.int32(0)))
    need = kf - chi
    ri = lax.broadcasted_iota(jnp.int32, (LANES, LANES), 0)
    ci = lax.broadcasted_iota(jnp.int32, (LANES, LANES), 1)
    tri = jnp.where(ri <= ci, 1.0, 0.0).astype(BF16)
    seen = jnp.zeros_like(need)
    j = jnp.full(lo.shape, NO_INDEX_LIMIT, jnp.int32)
    for t in range(sm.shape[1] // LANES):
        cols = slice(t * LANES, (t + 1) * LANES)
        band = jnp.logical_and(sm[:, cols] >= lo, sm[:, cols] < hi)
        cum = _dot(jnp.where(band, 1.0, 0.0).astype(BF16), tri) + seen
        over = jnp.logical_and(band, cum > need)
        j = jnp.minimum(j, jnp.min(jnp.where(over, kpos[:, cols], NO_INDEX_LIMIT), axis=-1, keepdims=True))
        seen = cum[:, LANES - 1:LANES]
    return lo, hi, jnp.where(clo > kf, j, jnp.int32(NO_INDEX_LIMIT))


ST_LO, ST_HI, ST_CLO, ST_CHI = range(4)


def _dsa_scores(wi_ref, qih_ref, kih_ref, sm_ref, st_ref, *, i, topk, nkeys):
    lane = lax.broadcasted_iota(jnp.int32, (Q_BLOCK, LANES), 1)
    sm_ref[:, 0:nkeys] = jnp.zeros((Q_BLOCK, nkeys), F32)

    def idx_head(h, carry):
        s = _dot(qih_ref[h], kih_ref[0:nkeys, :], _NT)
        w = jnp.sum(jnp.where(lane == IDX_DIM + h, wi_ref[...], 0.0), axis=-1, keepdims=True)
        sm_ref[:, 0:nkeys] += w * jnp.maximum(s, 0.0)
        return carry

    lax.fori_loop(0, IDX_HEADS, idx_head, 0)

    qpos = i * Q_BLOCK + lax.broadcasted_iota(jnp.int32, (Q_BLOCK, 1), 0)
    kpos = lax.broadcasted_iota(jnp.int32, (1, nkeys), 1)
    adm = _chunk_id(kpos) <= _chunk_id(qpos)
    sm = jnp.where(adm, sm_ref[:, 0:nkeys], -jnp.inf)
    sm_ref[:, 0:nkeys] = sm
    st = _select_fixed(sm, _count(adm), topk)
    for c, val in enumerate(st):
        st_ref[:, c:c + 1] = val


def _dsa_attend(o_ref, qh_ref, kh_ref, vh_ref, sm_ref, st_ref, j_ref, oh_ref, *, nkeys):
    kpos = lax.broadcasted_iota(jnp.int32, (1, nkeys), 1)
    sm = sm_ref[:, 0:nkeys]
    lo, hi = st_ref[:, ST_LO:ST_LO + 1], st_ref[:, ST_HI:ST_HI + 1]
    sel = jnp.logical_or(sm >= hi, jnp.logical_and(sm >= lo, kpos < j_ref[:, 0:1]))
    sm_ref[:, 0:nkeys] = jnp.where(sel, 0.0, -jnp.inf)
    group = A_HEADS // A_KV_HEADS

    def att_heads(t, carry):
        hs = [t * ATT_HEADS_PER_TRIP + d for d in range(ATT_HEADS_PER_TRIP)]
        ns = [h // group for h in hs]
        s = [_dot(qh_ref[h], kh_ref[n, 0:nkeys, :], _NT) + sm_ref[:, 0:nkeys] for h, n in zip(hs, ns)]
        m = [jnp.max(x, axis=-1, keepdims=True) for x in s]
        p = [jnp.exp(x - mx) for x, mx in zip(s, m)]
        l = [jnp.sum(x, axis=-1, keepdims=True) for x in p]
        o = [_dot(x.astype(BF16), vh_ref[n, 0:nkeys, :]) / lx for x, n, lx in zip(p, ns, l)]
        oh_ref[pl.ds(hs[0], ATT_HEADS_PER_TRIP)] = jnp.stack(o)
        return carry

    lax.fori_loop(0, A_HEADS // ATT_HEADS_PER_TRIP, att_heads, 0)
    o_ref[...] = jnp.concatenate([oh_ref[h] for h in range(A_HEADS)], axis=-1)


def _key_extents(nq):
    return sorted({min(nq, max(1, round(nq * f))) for f in KEY_EXTENT_FRACTIONS} | {nq})


def _dsa_kernel(q_ref, qi_ref, wi_ref, k_ref, v_ref, ki_ref, o_ref,
                qh_ref, qih_ref, kh_ref, vh_ref, kih_ref, sm_ref, st_ref, j_ref, oh_ref, *, topk, nq):
    i = pl.program_id(1)
    lp = nq * Q_BLOCK

    @pl.when(i == 0)
    def _():
        for n in range(A_KV_HEADS):
            kh_ref[n] = k_ref[:, n * A_HEAD_DIM:(n + 1) * A_HEAD_DIM]
            vh_ref[n] = v_ref[:, n * A_HEAD_DIM:(n + 1) * A_HEAD_DIM]
        kih_ref[...] = ki_ref[:, 0:IDX_DIM]

    scale = jnp.asarray(A_HEAD_DIM ** -0.5, BF16)
    for h in range(A_HEADS):
        qh_ref[h] = q_ref[:, h * A_HEAD_DIM:(h + 1) * A_HEAD_DIM] * scale
    for h in range(IDX_HEADS):
        qih_ref[h] = qi_ref[:, h * IDX_DIM:(h + 1) * IDX_DIM]

    last_chunk = 1 + ((i + 1) * Q_BLOCK - 1 - N_META) // CHUNK
    keys_end = N_META + CHUNK * last_chunk
    need = jnp.minimum((keys_end + Q_BLOCK - 1) // Q_BLOCK, nq)
    extents = _key_extents(nq)
    lows = [0] + extents[:-1]
    in_class = [jnp.logical_and(need > a, need <= e) for a, e in zip(lows, extents)]

    for e, on in zip(extents, in_class):
        @pl.when(on)
        def _(e=e):
            _dsa_scores(wi_ref, qih_ref, kih_ref, sm_ref, st_ref, i=i, topk=topk, nkeys=e * Q_BLOCK)

    j_ref[...] = jnp.full(j_ref.shape, NO_INDEX_LIMIT, jnp.int32)

    @pl.when(_any(st_ref[:, ST_CLO:ST_CLO + 1] > float(topk)))
    def _():
        nkeys = Q_BLOCK * sum(jnp.where(on, e, 0) for e, on in zip(extents, in_class))
        kpos = lax.broadcasted_iota(jnp.int32, (1, lp), 1)
        sm = jnp.where(kpos < nkeys, sm_ref[...], -jnp.inf)
        st = tuple(st_ref[:, c:c + 1] for c in range(4))
        lo, hi, j = _select_refine(sm, kpos, topk, st)
        st_ref[:, ST_LO:ST_LO + 1] = lo
        st_ref[:, ST_HI:ST_HI + 1] = hi
        j_ref[:, 0:1] = j

    for e, on in zip(extents, in_class):
        @pl.when(on)
        def _(e=e):
            _dsa_attend(o_ref, qh_ref, kh_ref, vh_ref, sm_ref, st_ref, j_ref, oh_ref, nkeys=e * Q_BLOCK)


def _dsa_call(ua, kiw, B, lp, topk):
    nq = lp // Q_BLOCK
    kern = functools.partial(_dsa_kernel, topk=topk, nq=nq)
    qrow = lambda b, i: b * nq + i
    return pl.pallas_call(
        kern,
        grid=(B, nq),
        in_specs=[pl.BlockSpec((Q_BLOCK, A_Q), lambda b, i: (qrow(b, i), 0)),
                  pl.BlockSpec((Q_BLOCK, IDX_Q), lambda b, i: (qrow(b, i), A_Q // IDX_Q)),
                  pl.BlockSpec((Q_BLOCK, LANES), lambda b, i: (qrow(b, i), 0)),
                  pl.BlockSpec((lp, A_KV), lambda b, i: (b, (A_Q + IDX_Q) // A_KV)),
                  pl.BlockSpec((lp, A_KV), lambda b, i: (b, (A_Q + IDX_Q) // A_KV + 1)),
                  pl.BlockSpec((lp, LANES), lambda b, i: (b, KIW_BLOCK))],
        out_specs=pl.BlockSpec((Q_BLOCK, A_Q), lambda b, i: (qrow(b, i), 0)),
        out_shape=jax.ShapeDtypeStruct((B * lp, A_Q), F32),
        scratch_shapes=[pltpu.VMEM((A_HEADS, Q_BLOCK, A_HEAD_DIM), BF16),
                        pltpu.VMEM((IDX_HEADS, Q_BLOCK, IDX_DIM), BF16),
                        pltpu.VMEM((A_KV_HEADS, lp, A_HEAD_DIM), BF16),
                        pltpu.VMEM((A_KV_HEADS, lp, A_HEAD_DIM), BF16),
                        pltpu.VMEM((lp, IDX_DIM), BF16),
                        pltpu.VMEM((Q_BLOCK, lp), F32),
                        pltpu.VMEM((Q_BLOCK, LANES), F32),
                        pltpu.VMEM((Q_BLOCK, LANES), jnp.int32),
                        pltpu.VMEM((A_HEADS, Q_BLOCK, A_HEAD_DIM), F32)],
        compiler_params=_params(("parallel", "arbitrary")),
        name="dsa",
    )(ua, ua, kiw, ua, ua, ua)


def _rwkv_a_kernel(ur_ref, mu_ref, w0_ref, wup_ref, a0_ref, aup_ref, gup_ref, kk_ref, ka_ref, rk_ref,
                   tril_ref, hsum_ref,
                   at_ref, bt_ref, kt_ref, rt_ref, v_ref, gc_ref, bonus_ref, g_ref, prev_ref):
    t = pl.program_id(1)
    tm = ur_ref.shape[0]

    @pl.when(t == 0)
    def _():
        prev_ref[...] = jnp.zeros_like(prev_ref)

    u = ur_ref[...]
    rows = lax.broadcasted_iota(jnp.int32, (tm, 1), 0)
    u_prev = jnp.where(rows == 0, prev_ref[0:1, :], pltpu.roll(u, 1, 0))
    prev_ref[0:1, :] = u[tm - 1:tm, :]
    u = u + (u_prev - u) * mu_ref[...]

    W = R_WIDTH
    r, k, v = u[:, 0:W], u[:, W:2 * W], u[:, 2 * W:3 * W]
    xw = u[:, 3 * W:3 * W + W_LORA]
    xa = u[:, 3 * W + W_LORA:3 * W + W_LORA + A_LORA]
    xg = u[:, 3 * W + W_LORA + A_LORA:]

    hp = lax.Precision.HIGHEST
    zw = w0_ref[...] + _dot(jnp.tanh(xw), wup_ref[...], precision=hp)
    logw = -(jnp.maximum(-zw, 0.0) + jnp.log1p(jnp.exp(-jnp.abs(zw)))) - 0.5
    lw = -jnp.exp(logw)
    a = jax.nn.sigmoid(a0_ref[...] + _dot(xa, aup_ref[...], precision=hp))
    g = _dot(jax.nn.sigmoid(xg), gup_ref[...], precision=hp)

    hsum = hsum_ref[...]
    kk = k * kk_ref[...]
    kk = kk * lax.rsqrt(jnp.maximum(_dot_split(kk * kk, hsum), 1e-24))
    kmod = k * (1.0 + (a - 1.0) * ka_ref[...])
    bonus_ref[...] = _dot_split(r * kmod * rk_ref[...], hsum) * v
    g_ref[...] = g
    v_ref[...] = v.astype(v_ref.dtype)

    cs = _dot(tril_ref[...], lw, precision=hp)
    e_neg = jnp.exp(-cs)
    at_ref[...] = (-kk * jnp.exp(cs - lw)).astype(at_ref.dtype)
    bt_ref[...] = (kk * a * e_neg).astype(bt_ref.dtype)
    kt_ref[...] = (kmod * e_neg).astype(kt_ref.dtype)
    rt_ref[...] = (r * jnp.exp(cs)).astype(rt_ref.dtype)
    for c in range(tm // RW_CHUNK):
        gc_ref[c] = jnp.exp(cs[(c + 1) * RW_CHUNK - 1:(c + 1) * RW_CHUNK, :])


def _rwkv_a_call(ur, B, lp, mu, w0, w_up, a0, a_up, g_up, k_k, k_a, r_k):
    T = B * lp
    tm = RWKV_A_TILE
    nt = lp // tm
    W = R_WIDTH
    ri = lax.broadcasted_iota(jnp.int32, (tm, tm), 0)
    ci = lax.broadcasted_iota(jnp.int32, (tm, tm), 1)
    tril = jnp.where(jnp.logical_and(ri >= ci, ri // RW_CHUNK == ci // RW_CHUNK), 1.0, 0.0).astype(F32)
    hr = lax.broadcasted_iota(jnp.int32, (W, W), 0) // R_HEAD_DIM
    hc = lax.broadcasted_iota(jnp.int32, (W, W), 1) // R_HEAD_DIM
    hsum = jnp.where(hr == hc, 1.0, 0.0).astype(BF16)
    vec = lambda x: x.reshape(1, -1)
    const = lambda shape: pl.BlockSpec(shape, lambda b, t: (0,) * len(shape))
    rowspec = lambda n: pl.BlockSpec((tm, n), lambda b, t: (b * nt + t, 0))
    nck = tm // RW_CHUNK
    out_rows = jax.ShapeDtypeStruct((T, W), F32)
    mxu_rows = jax.ShapeDtypeStruct((T, W), BF16)
    return pl.pallas_call(
        _rwkv_a_kernel,
        grid=(B, nt),
        in_specs=[rowspec(R_COLS), const((1, R_COLS)), const((1, W)), const((W_LORA, W)), const((1, W)),
                  const((A_LORA, W)), const((G_LORA, W)), const((1, W)), const((1, W)), const((1, W)),
                  const((tm, tm)), const((W, W))],
        out_specs=[rowspec(W), rowspec(W), rowspec(W), rowspec(W), rowspec(W),
                   pl.BlockSpec((nck, 1, W), lambda b, t: (b * nt + t, 0, 0)),
                   rowspec(W), rowspec(W)],
        out_shape=[mxu_rows, mxu_rows, mxu_rows, mxu_rows, mxu_rows,
                   jax.ShapeDtypeStruct((T // RW_CHUNK, 1, W), F32), out_rows, out_rows],
        scratch_shapes=[pltpu.VMEM((8, R_COLS), F32)],
        compiler_params=_params(("parallel", "arbitrary")),
        name="rwkv_a",
    )(ur, vec(mu), vec(w0), w_up, vec(a0), a_up, g_up, vec(k_k), vec(k_a), vec(r_k), tril, hsum)


def _rwkv_b_kernel(at_ref, bt_ref, kt_ref, rt_ref, v_ref, gc_ref, y_ref, s_ref):
    c = pl.program_id(1)

    @pl.when(c == 0)
    def _():
        s_ref[...] = jnp.zeros_like(s_ref)

    C = RW_CHUNK
    P = 2 * C
    lane = lax.broadcasted_iota(jnp.int32, (C, P), 1)
    lo = lane < R_HEAD_DIM
    ri = lax.broadcasted_iota(jnp.int32, (P, P), 0)
    ci = lax.broadcasted_iota(jnp.int32, (P, P), 1)
    same = (ri // C) == (ci // C)
    strict = jnp.logical_and(same, ri > ci)
    incl = jnp.logical_and(same, ri >= ci)
    eye = jnp.where(ri == ci, 1.0, 0.0).astype(F32)
    mm = _dot_bf16
    zero = jnp.zeros((), at_ref.dtype)

    def stack(x):
        return jnp.concatenate([jnp.where(lo, x, zero), jnp.where(lo, zero, x)], axis=0)

    pairs = range(R_HEADS // 2)
    sls = [slice(p * P, (p + 1) * P) for p in pairs]
    a_s = [stack(at_ref[:, sl]) for sl in sls]
    r_s = [stack(rt_ref[:, sl]) for sl in sls]
    v_s = [stack(v_ref[:, sl]) for sl in sls]
    bk_s = [jnp.concatenate([stack(bt_ref[:, sl]), stack(kt_ref[:, sl])], axis=0) for sl in sls]
    s0 = [s_ref[p] for p in pairs]
    g = [mm(jnp.concatenate([a_s[p], r_s[p]], axis=0), bk_s[p], _NT) for p in pairs]
    l_ab = [jnp.where(strict, g[p][0:P, 0:P], 0.0) for p in pairs]
    l_ak = [jnp.where(strict, g[p][0:P, P:2 * P], 0.0) for p in pairs]
    m_r = [jnp.where(jnp.concatenate([incl, incl], axis=1), g[p][P:2 * P, :], 0.0) for p in pairs]
    t_inv = [eye + l_ab[p] for p in pairs]
    pw = l_ab
    for _ in range(C.bit_length() - 2):
        pw = [mm(pw[p], pw[p]) for p in pairs]
        t_inv = [t_inv[p] + mm(t_inv[p], pw[p]) for p in pairs]
    z = [mm(a_s[p], s0[p], _NT) + mm(l_ak[p], v_s[p]) for p in pairs]
    uv = [jnp.concatenate([mm(t_inv[p], z[p]).astype(at_ref.dtype), v_s[p]], axis=0) for p in pairs]
    y_s = [mm(r_s[p], s0[p], _NT) + mm(m_r[p], uv[p]) for p in pairs]
    s_new = [(s0[p] + mm(uv[p], bk_s[p], _TN)) * gc_ref[0, :, sls[p]] for p in pairs]
    y_ref[...] = jnp.concatenate([y[0:C, :] + y[C:P, :] for y in y_s], axis=-1)
    s_ref[...] = jnp.stack(s_new, axis=0)


def _rwkv_b_call(at, bt, kt, rt, v, gc, B, lp):
    T, W = at.shape
    nc = lp // RW_CHUNK
    rowspec = pl.BlockSpec((RW_CHUNK, W), lambda b, c: (b * nc + c, 0))
    return pl.pallas_call(
        _rwkv_b_kernel,
        grid=(B, nc),
        in_specs=[rowspec, rowspec, rowspec, rowspec, rowspec,
                  pl.BlockSpec((1, 1, W), lambda b, c: (b * nc + c, 0, 0))],
        out_specs=rowspec,
        out_shape=jax.ShapeDtypeStruct((T, W), F32),
        scratch_shapes=[pltpu.VMEM((R_HEADS // 2, 2 * RW_CHUNK, 2 * R_HEAD_DIM), F32)],
        compiler_params=_params(("parallel", "arbitrary")),
        name="rwkv_b",
    )(at, bt, kt, rt, v, gc)


def _merge_kernel(h_ref, oa_ref, y_ref, bonus_ref, g_ref, ug_ref, bg_ref, gng_ref, gnb_ref, hmean_ref,
                  wa_ref, wb_ref, wo_ref, lng_ref, lnb_ref, o_ref, *, alpha):
    y = y_ref[...]
    hmean = hmean_ref[...]
    m = _dot_split(y, hmean)
    yc = y - m
    var = _dot_split(yc * yc, hmean)
    yn = yc * lax.rsqrt(var + GN_EPS) * gng_ref[...] + gnb_ref[...]
    ob = (yn + bonus_ref[...]) * g_ref[...]
    gates = jax.nn.sigmoid(ug_ref[...] + bg_ref[...])
    merged = (gates[:, 0:D_MODEL] * _dot_bf16(oa_ref[...], wa_ref[...])
              + gates[:, D_MODEL:] * _dot_bf16(ob, wb_ref[...]))
    mix = _dot_bf16(merged, wo_ref[...])
    o_ref[...] = _layer_norm(alpha * h_ref[...] + mix, lng_ref[...], lnb_ref[...])


def _merge_call(h, oa, y, bonus, g, ug, b_gate, gn_g, gn_b, wa, wb, wo, ln_g, ln_b, alpha):
    T, D = h.shape
    W = R_WIDTH
    hr = lax.broadcasted_iota(jnp.int32, (W, W), 0) // R_HEAD_DIM
    hc = lax.broadcasted_iota(jnp.int32, (W, W), 1) // R_HEAD_DIM
    hmean = jnp.where(hr == hc, 1.0 / R_HEAD_DIM, 0.0).astype(BF16)
    vec = lambda x: x.reshape(1, -1)
    row = lambda n: pl.BlockSpec((ROW_TILE, n), lambda i: (i, 0))
    const = lambda a, b: pl.BlockSpec((a, b), lambda i: (0, 0))
    return pl.pallas_call(
        functools.partial(_merge_kernel, alpha=alpha),
        grid=(T // ROW_TILE,),
        in_specs=[row(D), row(A_Q), row(W), row(W), row(W), row(GATE_COLS), const(1, GATE_COLS),
                  const(1, W), const(1, W), const(W, W), const(A_Q, D), const(W, D), const(D, D),
                  const(1, D), const(1, D)],
        out_specs=row(D),
        out_shape=jax.ShapeDtypeStruct((T, D), F32),
        compiler_params=_params(("parallel",)),
        name="merge",
    )(h, oa, y, bonus, g, ug, vec(b_gate), vec(gn_g), vec(gn_b), hmean,
      wa.astype(BF16), wb.astype(BF16), wo.astype(BF16), vec(ln_g), vec(ln_b))


MOE_TILES = (1536, 1024, 768, 512, 256)


def _route(logits_t, bias_t):
    s = jax.nn.sigmoid(logits_t)
    sel = s + bias_t
    row = lambda a, e: a[e:e + 1, :]
    gscore = []
    for gi in range(N_GROUPS):
        a, b, c, d = (row(sel, gi * EXPERTS_PER_GROUP + j) for j in range(EXPERTS_PER_GROUP))
        hi1, lo1 = jnp.maximum(a, b), jnp.minimum(a, b)
        hi2, lo2 = jnp.maximum(c, d), jnp.minimum(c, d)
        top1 = jnp.maximum(hi1, hi2)
        top2 = jnp.maximum(jnp.minimum(hi1, hi2), jnp.maximum(lo1, lo2))
        gscore.append(top1 + top2)
    chosen = []
    for gi in range(N_GROUPS):
        ok = None
        for gj in range(N_GROUPS):
            if gj == gi:
                continue
            t = gscore[gi] > gscore[gj] if gj < gi else gscore[gi] >= gscore[gj]
            ok = t if ok is None else jnp.logical_and(ok, t)
        chosen.append(ok)
    picked = []
    for e in range(N_EXPERTS):
        gi = e // EXPERTS_PER_GROUP
        rank = jnp.zeros_like(row(sel, e))
        for e2 in range(gi * EXPERTS_PER_GROUP, (gi + 1) * EXPERTS_PER_GROUP):
            if e2 == e:
                continue
            ahead = row(sel, e2) > row(sel, e) if e2 > e else row(sel, e2) >= row(sel, e)
            rank = rank + jnp.where(ahead, 1.0, 0.0)
        take = jnp.logical_and(chosen[gi], rank < 2.0)
        picked.append(jnp.where(take, row(s, e), 0.0))
    total = picked[0]
    for e in range(1, N_EXPERTS):
        total = total + picked[e]
    return jnp.concatenate(picked, axis=0) / total


def _moe_kernel(h_ref, wr_ref, br_ref, w1_ref, w3_ref, w2_ref, lng_ref, lnb_ref, o_ref,
                xb_ref, comb_ref, acc_ref, *, alpha):
    e = pl.program_id(1)

    @pl.when(e == 0)
    def _():
        x = h_ref[...]
        xb_ref[...] = x.astype(BF16)
        logits_t = _dot(wr_ref[...], x, _NT, precision=lax.Precision.HIGHEST)
        comb_t = _route(logits_t, br_ref[...])
        pad = jnp.zeros((LANES - N_EXPERTS, comb_t.shape[1]), F32)
        comb_ref[...] = jnp.transpose(jnp.concatenate([comb_t, pad], axis=0))
        acc_ref[...] = jnp.zeros_like(acc_ref)

    xb = xb_ref[...]
    h1 = _dot(xb, w1_ref[0].astype(BF16))
    h3 = _dot(xb, w3_ref[0].astype(BF16))
    act = (h1 * jax.nn.sigmoid(h1)) * h3
    he = _dot(act.astype(BF16), w2_ref[0].astype(BF16))
    lane = lax.broadcasted_iota(jnp.int32, comb_ref.shape, 1)
    ce = jnp.sum(jnp.where(lane == e, comb_ref[...], 0.0), axis=-1, keepdims=True)
    acc_ref[...] += ce * he

    @pl.when(e == pl.num_programs(1) - 1)
    def _():
        o_ref[...] = _layer_norm(alpha * h_ref[...] + acc_ref[...], lng_ref[...], lnb_ref[...])


def _moe_call(h, w_router_t, b_router, w1, w3, w2, ln_g, ln_b, alpha, tile):
    T, D = h.shape
    vec = lambda x: x.reshape(1, -1)
    return pl.pallas_call(
        functools.partial(_moe_kernel, alpha=alpha),
        grid=(T // tile, N_EXPERTS),
        in_specs=[pl.BlockSpec((tile, D), lambda i, e: (i, 0), pipeline_mode=pl.Buffered(1)),
                  pl.BlockSpec((N_EXPERTS, D), lambda i, e: (0, 0)),
                  pl.BlockSpec((N_EXPERTS, 1), lambda i, e: (0, 0)),
                  pl.BlockSpec((1, D, D_EXPERT), lambda i, e: (e, 0, 0)),
                  pl.BlockSpec((1, D, D_EXPERT), lambda i, e: (e, 0, 0)),
                  pl.BlockSpec((1, D_EXPERT, D), lambda i, e: (e, 0, 0)),
                  pl.BlockSpec((1, D), lambda i, e: (0, 0)),
                  pl.BlockSpec((1, D), lambda i, e: (0, 0))],
        out_specs=pl.BlockSpec((tile, D), lambda i, e: (i, 0)),
        out_shape=jax.ShapeDtypeStruct((T, D), F32),
        scratch_shapes=[pltpu.VMEM((tile, D), BF16), pltpu.VMEM((tile, LANES), F32),
                        pltpu.VMEM((tile, D), F32)],
        compiler_params=_params(("parallel", "arbitrary")),
        name="moe",
    )(h, w_router_t, b_router.reshape(N_EXPERTS, 1), w1, w3, w2, vec(ln_g), vec(ln_b))


def _split_w_in(w):
    o = 0
    q = w[:, o:o + A_Q]; o += A_Q
    k = w[:, o:o + A_KV]; o += A_KV
    v = w[:, o:o + A_KV]; o += A_KV
    qi = w[:, o:o + IDX_Q]; o += IDX_Q
    kiw = w[:, o:o + IDX_DIM + IDX_HEADS]; o += IDX_DIM + IDX_HEADS
    pad = jnp.zeros((w.shape[0], LANES - IDX_DIM - IDX_HEADS), w.dtype)
    wa = jnp.concatenate([q, qi, k, v, kiw, pad], axis=1)
    wr = w[:, o:o + R_COLS]; o += R_COLS
    wg = w[:, o:o + GATE_COLS]
    return wa.astype(BF16), wr.astype(BF16), wg.astype(BF16)


def _padded_len(L):
    return -(-L // Q_BLOCK) * Q_BLOCK


def kernel(x, meta_tokens, ln_in_g, ln_in_b, w_in, b_gate, rwkv_mu, rwkv_w0, rwkv_w_up, rwkv_a0, rwkv_a_up,
           rwkv_g_up, rwkv_k_k, rwkv_k_a, rwkv_r_k, rwkv_gn_g, rwkv_gn_b, w_branch_a, w_branch_b, w_out,
           ln1_g, ln1_b, ln2_g, ln2_b, w_router, b_router, w_exp1, w_exp3, w_exp2):
    B, S, D = x.shape
    depth = w_in.shape[0]
    alpha = (2 * depth) ** 0.25
    topk = min(TOPK_MAX, S // 4)
    L = N_META + S
    lp = _padded_len(L)
    T = B * lp
    assert T % ROW_TILE == 0
    moe_tile = next(t for t in MOE_TILES if T % t == 0)
    meta = jnp.broadcast_to(meta_tokens[None].astype(x.dtype), (B, N_META, D))
    hin = jnp.concatenate([meta, x, jnp.zeros((B, lp - L, D), x.dtype)], axis=1).reshape(T, D)
    h = _ln_call(hin, ln_in_g, ln_in_b)
    w_router_t = jnp.transpose(w_router)
    for l in range(depth):
        wa, wr, wg = _split_w_in(w_in[l])
        ua, kiw, ur, ug = _proj_call(h, wa, wr, wg)
        oa = _dsa_call(ua, kiw, B, lp, topk)
        at, bt, kt, rt, v, gc, bonus, g = _rwkv_a_call(
            ur, B, lp, rwkv_mu[l], rwkv_w0[l], rwkv_w_up[l], rwkv_a0[l], rwkv_a_up[l], rwkv_g_up[l],
            rwkv_k_k[l], rwkv_k_a[l], rwkv_r_k[l])
        y = _rwkv_b_call(at, bt, kt, rt, v, gc, B, lp)
        h = _merge_call(h, oa, y, bonus, g, ug, b_gate[l], rwkv_gn_g[l], rwkv_gn_b[l],
                        w_branch_a[l], w_branch_b[l], w_out[l], ln1_g[l], ln1_b[l], alpha)
        h = _moe_call(h, w_router_t, b_router, w_exp1[l], w_exp3[l], w_exp2[l], ln2_g[l], ln2_b[l], alpha,
                      moe_tile)
    return h.reshape(B, lp, D)[:, N_META:L]
```

```python
import functools

import jax
import jax.numpy as jnp
from jax import lax
from jax.experimental import pallas as pl
from jax.experimental.pallas import tpu as pltpu

D_MODEL = 1024
CHUNK = 64
N_META = 16
Q_BLOCK = 128
A_HEADS = 8
A_KV_HEADS = 2
A_HEAD_DIM = 64
IDX_HEADS = 8
IDX_DIM = 32
TOPK_MAX = 256
A_Q = A_HEADS * A_HEAD_DIM
A_KV = A_KV_HEADS * A_HEAD_DIM
IDX_Q = IDX_HEADS * IDX_DIM
R_HEADS = 8
R_HEAD_DIM = 64
R_WIDTH = R_HEADS * R_HEAD_DIM
W_LORA = 64
A_LORA = 64
G_LORA = 128
R_COLS = 3 * R_WIDTH + W_LORA + A_LORA + G_LORA
GN_EPS = 64e-5
GATE_COLS = 2 * D_MODEL
N_EXPERTS = 16
N_GROUPS = 4
EXPERTS_PER_GROUP = N_EXPERTS // N_GROUPS
D_EXPERT = 512
LN_EPS = 1e-5

LANES = 128
UA_COLS = A_Q + IDX_Q + 2 * A_KV + LANES
KIW_BLOCK = (A_Q + IDX_Q + 2 * A_KV) // LANES
ROW_TILE = 256
RWKV_A_TILES = (384, 256, 128)
RW_CHUNK = 64
VMEM_LIMIT = 56 * 1024 * 1024

F32 = jnp.float32
BF16 = jnp.bfloat16
INT_MIN = -2147483648


def _dot(a, b, dims=(((1,), (0,)), ((), ())), precision=None):
    return lax.dot_general(a, b, dims, precision=precision, preferred_element_type=F32)


def _dot_bf16(a, b, dims=(((1,), (0,)), ((), ()))):
    return _dot(a.astype(BF16), b.astype(BF16), dims)


_NT = (((1,), (1,)), ((), ()))
_TN = (((0,), (0,)), ((), ()))


def _dot_split(x, m_bf16):
    hi = x.astype(BF16)
    lo = (x - hi.astype(F32)).astype(BF16)
    return _dot(hi, m_bf16) + _dot(lo, m_bf16)


def _layer_norm(x, g, b):
    mu = jnp.mean(x, axis=-1, keepdims=True)
    xc = x - mu
    var = jnp.mean(xc * xc, axis=-1, keepdims=True)
    return xc * lax.rsqrt(var + LN_EPS) * g + b


def _params(sem):
    return pltpu.CompilerParams(dimension_semantics=sem, vmem_limit_bytes=VMEM_LIMIT)


def _ln_kernel(x_ref, g_ref, b_ref, o_ref):
    o_ref[...] = _layer_norm(x_ref[...], g_ref[...], b_ref[...])


def _ln_call(x, g, b):
    T, D = x.shape
    return pl.pallas_call(
        _ln_kernel,
        grid=(T // ROW_TILE,),
        in_specs=[pl.BlockSpec((ROW_TILE, D), lambda i: (i, 0)),
                  pl.BlockSpec((1, D), lambda i: (0, 0)),
                  pl.BlockSpec((1, D), lambda i: (0, 0))],
        out_specs=pl.BlockSpec((ROW_TILE, D), lambda i: (i, 0)),
        out_shape=jax.ShapeDtypeStruct((T, D), F32),
        compiler_params=_params(("parallel",)),
        name="ln_in",
    )(x, g.reshape(1, D), b.reshape(1, D))


def _proj_kernel(h_ref, wa_ref, wr_ref, wg_ref, ua_ref, kiw_ref, ur_ref, ug_ref):
    hb = h_ref[...].astype(BF16)
    ua = _dot(hb, wa_ref[...])
    ua_ref[...] = ua.astype(BF16)
    kiw_ref[...] = ua[:, UA_COLS - LANES:]
    ur_ref[...] = _dot(hb, wr_ref[...])
    ug_ref[...] = _dot(hb, wg_ref[...])


def _proj_call(h, wa, wr, wg):
    T, D = h.shape
    row = lambda n: pl.BlockSpec((ROW_TILE, n), lambda i: (i, 0))
    full = lambda n: pl.BlockSpec((D, n), lambda i: (0, 0))
    return pl.pallas_call(
        _proj_kernel,
        grid=(T // ROW_TILE,),
        in_specs=[row(D), full(UA_COLS), full(R_COLS), full(GATE_COLS)],
        out_specs=[row(UA_COLS), row(LANES), row(R_COLS), row(GATE_COLS)],
        out_shape=[jax.ShapeDtypeStruct((T, UA_COLS), BF16),
                   jax.ShapeDtypeStruct((T, LANES), F32),
                   jax.ShapeDtypeStruct((T, R_COLS), F32),
                   jax.ShapeDtypeStruct((T, GATE_COLS), F32)],
        compiler_params=_params(("parallel",)),
        name="proj",
    )(h, wa, wr, wg)


assert CHUNK & (CHUNK - 1) == 0
CHUNK_SHIFT = CHUNK.bit_length() - 1
BISECT_STEPS = 22
BISECT_ROW_GROUPS = 2
ATT_HEADS_PER_TRIP = 2
REFINE_CAP = 256
KEY_EXTENT_FRACTIONS = (0.18, 0.36, 0.64, 1.0)


def _chunk_id(p):
    return jnp.where(p < N_META, 0, 1 + ((p - N_META) >> CHUNK_SHIFT))


def _count(mask):
    return jnp.sum(jnp.where(mask, 1.0, 0.0), axis=-1, keepdims=True)


def _any(mask):
    return jnp.max(jnp.where(mask, 1.0, 0.0)) > 0.5


NO_INDEX_LIMIT = 2 ** 30


def _bisect_step(sm, kf, st):
    lo, hi, clo, chi = st
    mid = 0.5 * (lo + hi)
    cnt = _count(sm >= mid)
    ge = cnt >= kf
    return (jnp.where(ge, mid, lo), jnp.where(ge, hi, mid), jnp.where(ge, cnt, clo), jnp.where(ge, chi, cnt))


def _select_fixed(sm, n_adm, topk):
    kf = float(topk)
    mx = jnp.max(sm, axis=-1, keepdims=True)
    mn = jnp.min(jnp.where(sm == -jnp.inf, jnp.inf, sm), axis=-1, keepdims=True)
    hi0 = jnp.where(mx >= 0.0, 2.0 * mx + 1.0, 0.5 * mx + 1.0)
    st0 = (mn, hi0, n_adm, jnp.zeros_like(mn))
    rows = sm.shape[0] // BISECT_ROW_GROUPS
    parts = [slice(g * rows, (g + 1) * rows) for g in range(BISECT_ROW_GROUPS)]

    def body(_, sts):
        return tuple(_bisect_step(sm[p], kf, st) for p, st in zip(parts, sts))

    sts = lax.fori_loop(0, BISECT_STEPS, body, tuple(tuple(x[p] for x in st0) for p in parts), unroll=2)
    return tuple(jnp.concatenate([st[c] for st in sts], axis=0) for c in range(4))


def _select_refine(sm, kpos, topk, st):
    kf = float(topk)

    def open_rows(lo, hi, clo):
        band = jnp.logical_and(sm >= lo, sm < hi)
        bmax = jnp.max(jnp.where(band, sm, -jnp.inf), axis=-1, keepdims=True)
        bmin = jnp.min(jnp.where(band, sm, jnp.inf), axis=-1, keepdims=True)
        return _any(jnp.logical_and(clo > kf, bmax != bmin)).astype(jnp.int32)

    def body(c):
        lo, hi, clo, chi = _bisect_step(sm, kf, c[0:4])
        return lo, hi, clo, chi, open_rows(lo, hi, clo), c[5] + 1

    lo, hi, clo, chi = st
    lo, hi, clo, chi, _, _ = lax.while_loop(
        lambda c: jnp.logical_and(c[4] > 0, c[5] < REFINE_CAP), body,
        (lo, hi, clo, chi, open_rows(lo, hi, clo), jnp.int32(0)))
    need = kf - chi
    ri = lax.broadcasted_iota(jnp.int32, (LANES, LANES), 0)
    ci = lax.broadcasted_iota(jnp.int32, (LANES, LANES), 1)
    tri = jnp.where(ri <= ci, 1.0, 0.0).astype(BF16)
    seen = jnp.zeros_like(need)
    j = jnp.full(lo.shape, NO_INDEX_LIMIT, jnp.int32)
    for t in range(sm.shape[1] // LANES):
        cols = slice(t * LANES, (t + 1) * LANES)
        band = jnp.logical_and(sm[:, cols] >= lo, sm[:, cols] < hi)
        cum = _dot(jnp.where(band, 1.0, 0.0).astype(BF16), tri) + seen
        over = jnp.logical_and(band, cum > need)
        j = jnp.minimum(j, jnp.min(jnp.where(over, kpos[:, cols], NO_INDEX_LIMIT), axis=-1, keepdims=True))
        seen = cum[:, LANES - 1:LANES]
    return lo, hi, jnp.where(clo > kf, j, jnp.int32(NO_INDEX_LIMIT))


ST_LO, ST_HI, ST_CLO, ST_CHI = range(4)


def _dsa_scores(wi_ref, qih_ref, kih_ref, sm_ref, st_ref, *, i, topk, nkeys):
    lane = lax.broadcasted_iota(jnp.int32, (Q_BLOCK, LANES), 1)
    sm_ref[:, 0:nkeys] = jnp.zeros((Q_BLOCK, nkeys), F32)

    def idx_head(h, carry):
        s = _dot(qih_ref[h], kih_ref[0:nkeys, :], _NT)
        w = jnp.sum(jnp.where(lane == IDX_DIM + h, wi_ref[...], 0.0), axis=-1, keepdims=True)
        sm_ref[:, 0:nkeys] += w * jnp.maximum(s, 0.0)
        return carry

    lax.fori_loop(0, IDX_HEADS, idx_head, 0)

    qpos = i * Q_BLOCK + lax.broadcasted_iota(jnp.int32, (Q_BLOCK, 1), 0)
    kpos = lax.broadcasted_iota(jnp.int32, (1, nkeys), 1)
    adm = _chunk_id(kpos) <= _chunk_id(qpos)
    sm = jnp.where(adm, sm_ref[:, 0:nkeys], -jnp.inf)
    sm_ref[:, 0:nkeys] = sm
    st = _select_fixed(sm, _count(adm), topk)
    for c, val in enumerate(st):
        st_ref[:, c:c + 1] = val


def _dsa_attend(o_ref, qh_ref, kh_ref, vh_ref, sm_ref, st_ref, j_ref, oh_ref, *, nkeys):
    kpos = lax.broadcasted_iota(jnp.int32, (1, nkeys), 1)
    sm = sm_ref[:, 0:nkeys]
    lo, hi = st_ref[:, ST_LO:ST_LO + 1], st_ref[:, ST_HI:ST_HI + 1]
    sel = jnp.logical_or(sm >= hi, jnp.logical_and(sm >= lo, kpos < j_ref[:, 0:1]))
    sm_ref[:, 0:nkeys] = jnp.where(sel, 0.0, -jnp.inf)
    group = A_HEADS // A_KV_HEADS

    def att_heads(t, carry):
        hs = [t * ATT_HEADS_PER_TRIP + d for d in range(ATT_HEADS_PER_TRIP)]
        ns = [h // group for h in hs]
        s = [_dot(qh_ref[h], kh_ref[n, 0:nkeys, :], _NT) + sm_ref[:, 0:nkeys] for h, n in zip(hs, ns)]
        m = [jnp.max(x, axis=-1, keepdims=True) for x in s]
        p = [jnp.exp(x - mx) for x, mx in zip(s, m)]
        l = [jnp.sum(x, axis=-1, keepdims=True) for x in p]
        o = [_dot(x.astype(BF16), vh_ref[n, 0:nkeys, :]) / lx for x, n, lx in zip(p, ns, l)]
        oh_ref[pl.ds(hs[0], ATT_HEADS_PER_TRIP)] = jnp.stack(o)
        return carry

    lax.fori_loop(0, A_HEADS // ATT_HEADS_PER_TRIP, att_heads, 0)
    o_ref[...] = jnp.concatenate([oh_ref[h] for h in range(A_HEADS)], axis=-1)


def _key_extents(nq):
    return sorted({min(nq, max(1, round(nq * f))) for f in KEY_EXTENT_FRACTIONS} | {nq})


def _dsa_kernel(q_ref, qi_ref, wi_ref, k_ref, v_ref, ki_ref, o_ref,
                qh_ref, qih_ref, kh_ref, vh_ref, kih_ref, sm_ref, st_ref, j_ref, oh_ref, *, topk, nq):
    i = pl.program_id(1)
    lp = nq * Q_BLOCK

    @pl.when(i == 0)
    def _():
        for n in range(A_KV_HEADS):
            kh_ref[n] = k_ref[:, n * A_HEAD_DIM:(n + 1) * A_HEAD_DIM]
            vh_ref[n] = v_ref[:, n * A_HEAD_DIM:(n + 1) * A_HEAD_DIM]
        kih_ref[...] = ki_ref[:, 0:IDX_DIM]

    scale = jnp.asarray(A_HEAD_DIM ** -0.5, BF16)
    for h in range(A_HEADS):
        qh_ref[h] = q_ref[:, h * A_HEAD_DIM:(h + 1) * A_HEAD_DIM] * scale
    for h in range(IDX_HEADS):
        qih_ref[h] = qi_ref[:, h * IDX_DIM:(h + 1) * IDX_DIM]

    last_chunk = 1 + ((i + 1) * Q_BLOCK - 1 - N_META) // CHUNK
    keys_end = N_META + CHUNK * last_chunk
    need = jnp.minimum((keys_end + Q_BLOCK - 1) // Q_BLOCK, nq)
    extents = _key_extents(nq)
    lows = [0] + extents[:-1]
    in_class = [jnp.logical_and(need > a, need <= e) for a, e in zip(lows, extents)]

    for e, on in zip(extents, in_class):
        @pl.when(on)
        def _(e=e):
            _dsa_scores(wi_ref, qih_ref, kih_ref, sm_ref, st_ref, i=i, topk=topk, nkeys=e * Q_BLOCK)

    j_ref[...] = jnp.full(j_ref.shape, NO_INDEX_LIMIT, jnp.int32)

    @pl.when(_any(st_ref[:, ST_CLO:ST_CLO + 1] > float(topk)))
    def _():
        nkeys = Q_BLOCK * sum(jnp.where(on, e, 0) for e, on in zip(extents, in_class))
        kpos = lax.broadcasted_iota(jnp.int32, (1, lp), 1)
        sm = jnp.where(kpos < nkeys, sm_ref[...], -jnp.inf)
        st = tuple(st_ref[:, c:c + 1] for c in range(4))
        lo, hi, j = _select_refine(sm, kpos, topk, st)
        st_ref[:, ST_LO:ST_LO + 1] = lo
        st_ref[:, ST_HI:ST_HI + 1] = hi
        j_ref[:, 0:1] = j

    for e, on in zip(extents, in_class):
        @pl.when(on)
        def _(e=e):
            _dsa_attend(o_ref, qh_ref, kh_ref, vh_ref, sm_ref, st_ref, j_ref, oh_ref, nkeys=e * Q_BLOCK)


def _dsa_call(ua, kiw, B, lp, topk):
    nq = lp // Q_BLOCK
    kern = functools.partial(_dsa_kernel, topk=topk, nq=nq)
    qrow = lambda b, i: b * nq + i
    return pl.pallas_call(
        kern,
        grid=(B, nq),
        in_specs=[pl.BlockSpec((Q_BLOCK, A_Q), lambda b, i: (qrow(b, i), 0)),
                  pl.BlockSpec((Q_BLOCK, IDX_Q), lambda b, i: (qrow(b, i), A_Q // IDX_Q)),
                  pl.BlockSpec((Q_BLOCK, LANES), lambda b, i: (qrow(b, i), 0)),
                  pl.BlockSpec((lp, A_KV), lambda b, i: (b, (A_Q + IDX_Q) // A_KV)),
                  pl.BlockSpec((lp, A_KV), lambda b, i: (b, (A_Q + IDX_Q) // A_KV + 1)),
                  pl.BlockSpec((lp, LANES), lambda b, i: (b, KIW_BLOCK))],
        out_specs=pl.BlockSpec((Q_BLOCK, A_Q), lambda b, i: (qrow(b, i), 0)),
        out_shape=jax.ShapeDtypeStruct((B * lp, A_Q), F32),
        scratch_shapes=[pltpu.VMEM((A_HEADS, Q_BLOCK, A_HEAD_DIM), BF16),
                        pltpu.VMEM((IDX_HEADS, Q_BLOCK, IDX_DIM), BF16),
                        pltpu.VMEM((A_KV_HEADS, lp, A_HEAD_DIM), BF16),
                        pltpu.VMEM((A_KV_HEADS, lp, A_HEAD_DIM), BF16),
                        pltpu.VMEM((lp, IDX_DIM), BF16),
                        pltpu.VMEM((Q_BLOCK, lp), F32),
                        pltpu.VMEM((Q_BLOCK, LANES), F32),
                        pltpu.VMEM((Q_BLOCK, LANES), jnp.int32),
                        pltpu.VMEM((A_HEADS, Q_BLOCK, A_HEAD_DIM), F32)],
        compiler_params=_params(("parallel", "arbitrary")),
        name="dsa",
    )(ua, ua, kiw, ua, ua, ua)


def _rwkv_a_kernel(ur_ref, mu_ref, w0_ref, wup_ref, a0_ref, aup_ref, gup_ref, kk_ref, ka_ref, rk_ref,
                   tril_ref, hsum_ref,
                   at_ref, bt_ref, kt_ref, rt_ref, v_ref, gc_ref, bonus_ref, g_ref, prev_ref):
    t = pl.program_id(1)
    tm = ur_ref.shape[0]

    @pl.when(t == 0)
    def _():
        prev_ref[...] = jnp.zeros_like(prev_ref)

    u = ur_ref[...]
    rows = lax.broadcasted_iota(jnp.int32, (tm, 1), 0)
    u_prev = jnp.where(rows == 0, prev_ref[0:1, :], pltpu.roll(u, 1, 0))
    prev_ref[0:1, :] = u[tm - 1:tm, :]
    u = u + (u_prev - u) * mu_ref[...]

    W = R_WIDTH
    r, k, v = u[:, 0:W], u[:, W:2 * W], u[:, 2 * W:3 * W]
    xw = u[:, 3 * W:3 * W + W_LORA]
    xa = u[:, 3 * W + W_LORA:3 * W + W_LORA + A_LORA]
    xg = u[:, 3 * W + W_LORA + A_LORA:]

    hp = lax.Precision.HIGHEST
    zw = w0_ref[...] + _dot(jnp.tanh(xw), wup_ref[...], precision=hp)
    logw = -(jnp.maximum(-zw, 0.0) + jnp.log1p(jnp.exp(-jnp.abs(zw)))) - 0.5
    lw = -jnp.exp(logw)
    a = jax.nn.sigmoid(a0_ref[...] + _dot(xa, aup_ref[...], precision=hp))
    g = _dot(jax.nn.sigmoid(xg), gup_ref[...], precision=hp)

    hsum = hsum_ref[...]
    kk = k * kk_ref[...]
    kk = kk * lax.rsqrt(jnp.maximum(_dot_split(kk * kk, hsum), 1e-24))
    kmod = k * (1.0 + (a - 1.0) * ka_ref[...])
    bonus_ref[...] = _dot_split(r * kmod * rk_ref[...], hsum) * v
    g_ref[...] = g
    v_ref[...] = v.astype(v_ref.dtype)

    cs = _dot(tril_ref[...], lw, precision=hp)
    e_neg = jnp.exp(-cs)
    at_ref[...] = (-kk * jnp.exp(cs - lw)).astype(at_ref.dtype)
    bt_ref[...] = (kk * a * e_neg).astype(bt_ref.dtype)
    kt_ref[...] = (kmod * e_neg).astype(kt_ref.dtype)
    rt_ref[...] = (r * jnp.exp(cs)).astype(rt_ref.dtype)
    for c in range(tm // RW_CHUNK):
        gc_ref[c] = jnp.exp(cs[(c + 1) * RW_CHUNK - 1:(c + 1) * RW_CHUNK, :])


def _rwkv_a_call(ur, B, lp, mu, w0, w_up, a0, a_up, g_up, k_k, k_a, r_k):
    T = B * lp
    tm = next(t for t in RWKV_A_TILES if lp % t == 0)
    nt = lp // tm
    W = R_WIDTH
    ri = lax.broadcasted_iota(jnp.int32, (tm, tm), 0)
    ci = lax.broadcasted_iota(jnp.int32, (tm, tm), 1)
    tril = jnp.where(jnp.logical_and(ri >= ci, ri // RW_CHUNK == ci // RW_CHUNK), 1.0, 0.0).astype(F32)
    hr = lax.broadcasted_iota(jnp.int32, (W, W), 0) // R_HEAD_DIM
    hc = lax.broadcasted_iota(jnp.int32, (W, W), 1) // R_HEAD_DIM
    hsum = jnp.where(hr == hc, 1.0, 0.0).astype(BF16)
    vec = lambda x: x.reshape(1, -1)
    const = lambda shape: pl.BlockSpec(shape, lambda b, t: (0,) * len(shape))
    rowspec = lambda n: pl.BlockSpec((tm, n), lambda b, t: (b * nt + t, 0))
    nck = tm // RW_CHUNK
    out_rows = jax.ShapeDtypeStruct((T, W), F32)
    mxu_rows = jax.ShapeDtypeStruct((T, W), BF16)
    return pl.pallas_call(
        _rwkv_a_kernel,
        grid=(B, nt),
        in_specs=[rowspec(R_COLS), const((1, R_COLS)), const((1, W)), const((W_LORA, W)), const((1, W)),
                  const((A_LORA, W)), const((G_LORA, W)), const((1, W)), const((1, W)), const((1, W)),
                  const((tm, tm)), const((W, W))],
        out_specs=[rowspec(W), rowspec(W), rowspec(W), rowspec(W), rowspec(W),
                   pl.BlockSpec((nck, 1, W), lambda b, t: (b * nt + t, 0, 0)),
                   rowspec(W), rowspec(W)],
        out_shape=[mxu_rows, mxu_rows, mxu_rows, mxu_rows, mxu_rows,
                   jax.ShapeDtypeStruct((T // RW_CHUNK, 1, W), F32), out_rows, out_rows],
        scratch_shapes=[pltpu.VMEM((8, R_COLS), F32)],
        compiler_params=_params(("parallel", "arbitrary")),
        name="rwkv_a",
    )(ur, vec(mu), vec(w0), w_up, vec(a0), a_up, g_up, vec(k_k), vec(k_a), vec(r_k), tril, hsum)


def _rwkv_b_kernel(at_ref, bt_ref, kt_ref, rt_ref, v_ref, gc_ref, y_ref, s_ref):
    c = pl.program_id(1)

    @pl.when(c == 0)
    def _():
        s_ref[...] = jnp.zeros_like(s_ref)

    C = RW_CHUNK
    P = 2 * C
    lane = lax.broadcasted_iota(jnp.int32, (C, P), 1)
    lo = lane < R_HEAD_DIM
    ri = lax.broadcasted_iota(jnp.int32, (P, P), 0)
    ci = lax.broadcasted_iota(jnp.int32, (P, P), 1)
    same = (ri // C) == (ci // C)
    strict = jnp.logical_and(same, ri > ci)
    incl = jnp.logical_and(same, ri >= ci)
    eye = jnp.where(ri == ci, 1.0, 0.0).astype(F32)
    mm = _dot_bf16
    zero = jnp.zeros((), at_ref.dtype)

    def stack(x):
        return jnp.concatenate([jnp.where(lo, x, zero), jnp.where(lo, zero, x)], axis=0)

    pairs = range(R_HEADS // 2)
    sls = [slice(p * P, (p + 1) * P) for p in pairs]
    a_s = [stack(at_ref[:, sl]) for sl in sls]
    r_s = [stack(rt_ref[:, sl]) for sl in sls]
    v_s = [stack(v_ref[:, sl]) for sl in sls]
    bk_s = [jnp.concatenate([stack(bt_ref[:, sl]), stack(kt_ref[:, sl])], axis=0) for sl in sls]
    s0 = [s_ref[p] for p in pairs]
    g = [mm(jnp.concatenate([a_s[p], r_s[p]], axis=0), bk_s[p], _NT) for p in pairs]
    l_ab = [jnp.where(strict, g[p][0:P, 0:P], 0.0) for p in pairs]
    l_ak = [jnp.where(strict, g[p][0:P, P:2 * P], 0.0) for p in pairs]
    m_r = [jnp.where(jnp.concatenate([incl, incl], axis=1), g[p][P:2 * P, :], 0.0) for p in pairs]
    t_inv = [eye + l_ab[p] for p in pairs]
    pw = l_ab
    for _ in range(C.bit_length() - 2):
        pw = [mm(pw[p], pw[p]) for p in pairs]
        t_inv = [t_inv[p] + mm(t_inv[p], pw[p]) for p in pairs]
    z = [mm(a_s[p], s0[p], _NT) + mm(l_ak[p], v_s[p]) for p in pairs]
    uv = [jnp.concatenate([mm(t_inv[p], z[p]).astype(at_ref.dtype), v_s[p]], axis=0) for p in pairs]
    y_s = [mm(r_s[p], s0[p], _NT) + mm(m_r[p], uv[p]) for p in pairs]
    s_new = [(s0[p] + mm(uv[p], bk_s[p], _TN)) * gc_ref[0, :, sls[p]] for p in pairs]
    y_ref[...] = jnp.concatenate([y[0:C, :] + y[C:P, :] for y in y_s], axis=-1)
    s_ref[...] = jnp.stack(s_new, axis=0)


def _rwkv_b_call(at, bt, kt, rt, v, gc, B, lp):
    T, W = at.shape
    nc = lp // RW_CHUNK
    rowspec = pl.BlockSpec((RW_CHUNK, W), lambda b, c: (b * nc + c, 0))
    return pl.pallas_call(
        _rwkv_b_kernel,
        grid=(B, nc),
        in_specs=[rowspec, rowspec, rowspec, rowspec, rowspec,
                  pl.BlockSpec((1, 1, W), lambda b, c: (b * nc + c, 0, 0))],
        out_specs=rowspec,
        out_shape=jax.ShapeDtypeStruct((T, W), F32),
        scratch_shapes=[pltpu.VMEM((R_HEADS // 2, 2 * RW_CHUNK, 2 * R_HEAD_DIM), F32)],
        compiler_params=_params(("parallel", "arbitrary")),
        name="rwkv_b",
    )(at, bt, kt, rt, v, gc)


def _merge_kernel(h_ref, oa_ref, y_ref, bonus_ref, g_ref, ug_ref, bg_ref, gng_ref, gnb_ref, hmean_ref,
                  wa_ref, wb_ref, wo_ref, lng_ref, lnb_ref, o_ref, *, alpha):
    y = y_ref[...]
    hmean = hmean_ref[...]
    m = _dot_split(y, hmean)
    yc = y - m
    var = _dot_split(yc * yc, hmean)
    yn = yc * lax.rsqrt(var + GN_EPS) * gng_ref[...] + gnb_ref[...]
    ob = (yn + bonus_ref[...]) * g_ref[...]
    gates = jax.nn.sigmoid(ug_ref[...] + bg_ref[...])
    merged = (gates[:, 0:D_MODEL] * _dot_bf16(oa_ref[...], wa_ref[...])
              + gates[:, D_MODEL:] * _dot_bf16(ob, wb_ref[...]))
    mix = _dot_bf16(merged, wo_ref[...])
    o_ref[...] = _layer_norm(alpha * h_ref[...] + mix, lng_ref[...], lnb_ref[...])


def _merge_call(h, oa, y, bonus, g, ug, b_gate, gn_g, gn_b, wa, wb, wo, ln_g, ln_b, alpha):
    T, D = h.shape
    W = R_WIDTH
    hr = lax.broadcasted_iota(jnp.int32, (W, W), 0) // R_HEAD_DIM
    hc = lax.broadcasted_iota(jnp.int32, (W, W), 1) // R_HEAD_DIM
    hmean = jnp.where(hr == hc, 1.0 / R_HEAD_DIM, 0.0).astype(BF16)
    vec = lambda x: x.reshape(1, -1)
    row = lambda n: pl.BlockSpec((ROW_TILE, n), lambda i: (i, 0))
    const = lambda a, b: pl.BlockSpec((a, b), lambda i: (0, 0))
    return pl.pallas_call(
        functools.partial(_merge_kernel, alpha=alpha),
        grid=(T // ROW_TILE,),
        in_specs=[row(D), row(A_Q), row(W), row(W), row(W), row(GATE_COLS), const(1, GATE_COLS),
                  const(1, W), const(1, W), const(W, W), const(A_Q, D), const(W, D), const(D, D),
                  const(1, D), const(1, D)],
        out_specs=row(D),
        out_shape=jax.ShapeDtypeStruct((T, D), F32),
        compiler_params=_params(("parallel",)),
        name="merge",
    )(h, oa, y, bonus, g, ug, vec(b_gate), vec(gn_g), vec(gn_b), hmean,
      wa.astype(BF16), wb.astype(BF16), wo.astype(BF16), vec(ln_g), vec(ln_b))


MOE_TILES = (1536, 1024, 768, 512, 256)


def _route(logits_t, bias_t):
    s = jax.nn.sigmoid(logits_t)
    sel = s + bias_t
    row = lambda a, e: a[e:e + 1, :]
    gscore = []
    for gi in range(N_GROUPS):
        a, b, c, d = (row(sel, gi * EXPERTS_PER_GROUP + j) for j in range(EXPERTS_PER_GROUP))
        hi1, lo1 = jnp.maximum(a, b), jnp.minimum(a, b)
        hi2, lo2 = jnp.maximum(c, d), jnp.minimum(c, d)
        top1 = jnp.maximum(hi1, hi2)
        top2 = jnp.maximum(jnp.minimum(hi1, hi2), jnp.maximum(lo1, lo2))
        gscore.append(top1 + top2)
    chosen = []
    for gi in range(N_GROUPS):
        ok = None
        for gj in range(N_GROUPS):
            if gj == gi:
                continue
            t = gscore[gi] > gscore[gj] if gj < gi else gscore[gi] >= gscore[gj]
            ok = t if ok is None else jnp.logical_and(ok, t)
        chosen.append(ok)
    picked = []
    for e in range(N_EXPERTS):
        gi = e // EXPERTS_PER_GROUP
        rank = jnp.zeros_like(row(sel, e))
        for e2 in range(gi * EXPERTS_PER_GROUP, (gi + 1) * EXPERTS_PER_GROUP):
            if e2 == e:
                continue
            ahead = row(sel, e2) > row(sel, e) if e2 > e else row(sel, e2) >= row(sel, e)
            rank = rank + jnp.where(ahead, 1.0, 0.0)
        take = jnp.logical_and(chosen[gi], rank < 2.0)
        picked.append(jnp.where(take, row(s, e), 0.0))
    total = picked[0]
    for e in range(1, N_EXPERTS):
        total = total + picked[e]
    return jnp.concatenate(picked, axis=0) / total


def _moe_kernel(h_ref, wr_ref, br_ref, w1_ref, w3_ref, w2_ref, lng_ref, lnb_ref, o_ref,
                xb_ref, comb_ref, acc_ref, *, alpha):
    e = pl.program_id(1)

    @pl.when(e == 0)
    def _():
        x = h_ref[...]
        xb_ref[...] = x.astype(BF16)
        logits_t = _dot(wr_ref[...], x, _NT, precision=lax.Precision.HIGHEST)
        comb_t = _route(logits_t, br_ref[...])
        pad = jnp.zeros((LANES - N_EXPERTS, comb_t.shape[1]), F32)
        comb_ref[...] = jnp.transpose(jnp.concatenate([comb_t, pad], axis=0))
        acc_ref[...] = jnp.zeros_like(acc_ref)

    xb = xb_ref[...]
    h1 = _dot(xb, w1_ref[0].astype(BF16))
    h3 = _dot(xb, w3_ref[0].astype(BF16))
    act = (h1 * jax.nn.sigmoid(h1)) * h3
    he = _dot(act.astype(BF16), w2_ref[0].astype(BF16))
    lane = lax.broadcasted_iota(jnp.int32, comb_ref.shape, 1)
    ce = jnp.sum(jnp.where(lane == e, comb_ref[...], 0.0), axis=-1, keepdims=True)
    acc_ref[...] += ce * he

    @pl.when(e == pl.num_programs(1) - 1)
    def _():
        o_ref[...] = _layer_norm(alpha * h_ref[...] + acc_ref[...], lng_ref[...], lnb_ref[...])


def _moe_call(h, w_router_t, b_router, w1, w3, w2, ln_g, ln_b, alpha, tile):
    T, D = h.shape
    vec = lambda x: x.reshape(1, -1)
    return pl.pallas_call(
        functools.partial(_moe_kernel, alpha=alpha),
        grid=(T // tile, N_EXPERTS),
        in_specs=[pl.BlockSpec((tile, D), lambda i, e: (i, 0), pipeline_mode=pl.Buffered(1)),
                  pl.BlockSpec((N_EXPERTS, D), lambda i, e: (0, 0)),
                  pl.BlockSpec((N_EXPERTS, 1), lambda i, e: (0, 0)),
                  pl.BlockSpec((1, D, D_EXPERT), lambda i, e: (e, 0, 0)),
                  pl.BlockSpec((1, D, D_EXPERT), lambda i, e: (e, 0, 0)),
                  pl.BlockSpec((1, D_EXPERT, D), lambda i, e: (e, 0, 0)),
                  pl.BlockSpec((1, D), lambda i, e: (0, 0)),
                  pl.BlockSpec((1, D), lambda i, e: (0, 0))],
        out_specs=pl.BlockSpec((tile, D), lambda i, e: (i, 0)),
        out_shape=jax.ShapeDtypeStruct((T, D), F32),
        scratch_shapes=[pltpu.VMEM((tile, D), BF16), pltpu.VMEM((tile, LANES), F32),
                        pltpu.VMEM((tile, D), F32)],
        compiler_params=_params(("parallel", "arbitrary")),
        name="moe",
    )(h, w_router_t, b_router.reshape(N_EXPERTS, 1), w1, w3, w2, vec(ln_g), vec(ln_b))


def _split_w_in(w):
    o = 0
    q = w[:, o:o + A_Q]; o += A_Q
    k = w[:, o:o + A_KV]; o += A_KV
    v = w[:, o:o + A_KV]; o += A_KV
    qi = w[:, o:o + IDX_Q]; o += IDX_Q
    kiw = w[:, o:o + IDX_DIM + IDX_HEADS]; o += IDX_DIM + IDX_HEADS
    pad = jnp.zeros((w.shape[0], LANES - IDX_DIM - IDX_HEADS), w.dtype)
    wa = jnp.concatenate([q, qi, k, v, kiw, pad], axis=1)
    wr = w[:, o:o + R_COLS]; o += R_COLS
    wg = w[:, o:o + GATE_COLS]
    return wa.astype(BF16), wr.astype(BF16), wg.astype(BF16)


def _padded_len(L):
    return -(-L // Q_BLOCK) * Q_BLOCK


def kernel(x, meta_tokens, ln_in_g, ln_in_b, w_in, b_gate, rwkv_mu, rwkv_w0, rwkv_w_up, rwkv_a0, rwkv_a_up,
           rwkv_g_up, rwkv_k_k, rwkv_k_a, rwkv_r_k, rwkv_gn_g, rwkv_gn_b, w_branch_a, w_branch_b, w_out,
           ln1_g, ln1_b, ln2_g, ln2_b, w_router, b_router, w_exp1, w_exp3, w_exp2):
    B, S, D = x.shape
    depth = w_in.shape[0]
    alpha = (2 * depth) ** 0.25
    topk = min(TOPK_MAX, S // 4)
    L = N_META + S
    lp = _padded_len(L)
    T = B * lp
    assert T % ROW_TILE == 0
    moe_tile = next(t for t in MOE_TILES if T % t == 0)
    meta = jnp.broadcast_to(meta_tokens[None].astype(x.dtype), (B, N_META, D))
    hin = jnp.concatenate([meta, x, jnp.zeros((B, lp - L, D), x.dtype)], axis=1).reshape(T, D)
    h = _ln_call(hin, ln_in_g, ln_in_b)
    w_router_t = jnp.transpose(w_router)
    for l in range(depth):
        wa, wr, wg = _split_w_in(w_in[l])
        ua, kiw, ur, ug = _proj_call(h, wa, wr, wg)
        oa = _dsa_call(ua, kiw, B, lp, topk)
        at, bt, kt, rt, v, gc, bonus, g = _rwkv_a_call(
            ur, B, lp, rwkv_mu[l], rwkv_w0[l], rwkv_w_up[l], rwkv_a0[l], rwkv_a_up[l], rwkv_g_up[l],
            rwkv_k_k[l], rwkv_k_a[l], rwkv_r_k[l])
        y = _rwkv_b_call(at, bt, kt, rt, v, gc, B, lp)
        h = _merge_call(h, oa, y, bonus, g, ug, b_gate[l], rwkv_gn_g[l], rwkv_gn_b[l],
                        w_branch_a[l], w_branch_b[l], w_out[l], ln1_g[l], ln1_b[l], alpha)
        h = _moe_call(h, w_router_t, b_router, w_exp1[l], w_exp3[l], w_exp2[l], ln2_g[l], ln2_b[l], alpha,
                      moe_tile)
    return h.reshape(B, lp, D)[:, N_META:L]
```

```python
import functools

import jax
import jax.numpy as jnp
from jax import lax
from jax.experimental import pallas as pl
from jax.experimental.pallas import tpu as pltpu

D_MODEL = 1024
CHUNK = 64
N_META = 16
Q_BLOCK = 128
A_HEADS = 8
A_KV_HEADS = 2
A_HEAD_DIM = 64
IDX_HEADS = 8
IDX_DIM = 32
TOPK_MAX = 256
A_Q = A_HEADS * A_HEAD_DIM
A_KV = A_KV_HEADS * A_HEAD_DIM
IDX_Q = IDX_HEADS * IDX_DIM
R_HEADS = 8
R_HEAD_DIM = 64
R_WIDTH = R_HEADS * R_HEAD_DIM
W_LORA = 64
A_LORA = 64
G_LORA = 128
R_COLS = 3 * R_WIDTH + W_LORA + A_LORA + G_LORA
GN_EPS = 64e-5
GATE_COLS = 2 * D_MODEL
N_EXPERTS = 16
N_GROUPS = 4
EXPERTS_PER_GROUP = N_EXPERTS // N_GROUPS
D_EXPERT = 512
LN_EPS = 1e-5

LANES = 128
UA_COLS = A_Q + IDX_Q + 2 * A_KV + LANES
KIW_BLOCK = (A_Q + IDX_Q + 2 * A_KV) // LANES
ROW_TILE = 256
RWKV_A_TILES = (384, 256, 128)
RW_CHUNK = 64
VMEM_LIMIT = 56 * 1024 * 1024

F32 = jnp.float32
BF16 = jnp.bfloat16
INT_MIN = -2147483648


def _dot(a, b, dims=(((1,), (0,)), ((), ())), precision=None):
    return lax.dot_general(a, b, dims, precision=precision, preferred_element_type=F32)


def _dot_bf16(a, b, dims=(((1,), (0,)), ((), ()))):
    return _dot(a.astype(BF16), b.astype(BF16), dims)


_NT = (((1,), (1,)), ((), ()))
_TN = (((0,), (0,)), ((), ()))


def _dot_split(x, m_bf16):
    hi = x.astype(BF16)
    lo = (x - hi.astype(F32)).astype(BF16)
    return _dot(hi, m_bf16) + _dot(lo, m_bf16)


def _layer_norm(x, g, b):
    mu = jnp.mean(x, axis=-1, keepdims=True)
    xc = x - mu
    var = jnp.mean(xc * xc, axis=-1, keepdims=True)
    return xc * lax.rsqrt(var + LN_EPS) * g + b


def _params(sem):
    return pltpu.CompilerParams(dimension_semantics=sem, vmem_limit_bytes=VMEM_LIMIT)


def _ln_kernel(x_ref, g_ref, b_ref, o_ref):
    o_ref[...] = _layer_norm(x_ref[...], g_ref[...], b_ref[...])


def _ln_call(x, g, b):
    T, D = x.shape
    return pl.pallas_call(
        _ln_kernel,
        grid=(T // ROW_TILE,),
        in_specs=[pl.BlockSpec((ROW_TILE, D), lambda i: (i, 0)),
                  pl.BlockSpec((1, D), lambda i: (0, 0)),
                  pl.BlockSpec((1, D), lambda i: (0, 0))],
        out_specs=pl.BlockSpec((ROW_TILE, D), lambda i: (i, 0)),
        out_shape=jax.ShapeDtypeStruct((T, D), F32),
        compiler_params=_params(("parallel",)),
        name="ln_in",
    )(x, g.reshape(1, D), b.reshape(1, D))


def _proj_kernel(h_ref, wa_ref, wr_ref, wg_ref, ua_ref, kiw_ref, ur_ref, ug_ref):
    hb = h_ref[...].astype(BF16)
    ua = _dot(hb, wa_ref[...])
    ua_ref[...] = ua.astype(BF16)
    kiw_ref[...] = ua[:, UA_COLS - LANES:]
    ur_ref[...] = _dot(hb, wr_ref[...])
    ug_ref[...] = _dot(hb, wg_ref[...])


def _proj_call(h, wa, wr, wg):
    T, D = h.shape
    row = lambda n: pl.BlockSpec((ROW_TILE, n), lambda i: (i, 0))
    full = lambda n: pl.BlockSpec((D, n), lambda i: (0, 0))
    return pl.pallas_call(
        _proj_kernel,
        grid=(T // ROW_TILE,),
        in_specs=[row(D), full(UA_COLS), full(R_COLS), full(GATE_COLS)],
        out_specs=[row(UA_COLS), row(LANES), row(R_COLS), row(GATE_COLS)],
        out_shape=[jax.ShapeDtypeStruct((T, UA_COLS), BF16),
                   jax.ShapeDtypeStruct((T, LANES), F32),
                   jax.ShapeDtypeStruct((T, R_COLS), F32),
                   jax.ShapeDtypeStruct((T, GATE_COLS), F32)],
        compiler_params=_params(("parallel",)),
        name="proj",
    )(h, wa, wr, wg)


assert CHUNK & (CHUNK - 1) == 0
CHUNK_SHIFT = CHUNK.bit_length() - 1
BISECT_STEPS = 22
BISECT_ROW_GROUPS = 2
ATT_HEADS_PER_TRIP = 2
REFINE_CAP = 256
KEY_EXTENT_FRACTIONS = (0.18, 0.36, 0.64, 1.0)


def _chunk_id(p):
    return jnp.where(p < N_META, 0, 1 + ((p - N_META) >> CHUNK_SHIFT))


def _count(mask):
    return jnp.sum(jnp.where(mask, 1.0, 0.0), axis=-1, keepdims=True)


def _any(mask):
    return jnp.max(jnp.where(mask, 1.0, 0.0)) > 0.5


NO_INDEX_LIMIT = 2 ** 30


def _bisect_step(sm, kf, st):
    lo, hi, clo, chi = st
    mid = 0.5 * (lo + hi)
    cnt = _count(sm >= mid)
    ge = cnt >= kf
    return (jnp.where(ge, mid, lo), jnp.where(ge, hi, mid), jnp.where(ge, cnt, clo), jnp.where(ge, chi, cnt))


def _select_fixed(sm, n_adm, topk):
    kf = float(topk)
    mx = jnp.max(sm, axis=-1, keepdims=True)
    mn = jnp.min(jnp.where(sm == -jnp.inf, jnp.inf, sm), axis=-1, keepdims=True)
    hi0 = jnp.where(mx >= 0.0, 2.0 * mx + 1.0, 0.5 * mx + 1.0)
    st0 = (mn, hi0, n_adm, jnp.zeros_like(mn))
    rows = sm.shape[0] // BISECT_ROW_GROUPS
    parts = [slice(g * rows, (g + 1) * rows) for g in range(BISECT_ROW_GROUPS)]

    def body(_, sts):
        return tuple(_bisect_step(sm[p], kf, st) for p, st in zip(parts, sts))

    sts = lax.fori_loop(0, BISECT_STEPS, body, tuple(tuple(x[p] for x in st0) for p in parts), unroll=2)
    return tuple(jnp.concatenate([st[c] for st in sts], axis=0) for c in range(4))


def _select_refine(sm, kpos, topk, st):
    kf = float(topk)

    def open_rows(lo, hi, clo):
        band = jnp.logical_and(sm >= lo, sm < hi)
        bmax = jnp.max(jnp.where(band, sm, -jnp.inf), axis=-1, keepdims=True)
        bmin = jnp.min(jnp.where(band, sm, jnp.inf), axis=-1, keepdims=True)
        return _any(jnp.logical_and(clo > kf, bmax != bmin)).astype(jnp.int32)

    def body(c):
        lo, hi, clo, chi = _bisect_step(sm, kf, c[0:4])
        return lo, hi, clo, chi, open_rows(lo, hi, clo), c[5] + 1

    lo, hi, clo, chi = st
    lo, hi, clo, chi, _, _ = lax.while_loop(
        lambda c: jnp.logical_and(c[4] > 0, c[5] < REFINE_CAP), body,
        (lo, hi, clo, chi, open_rows(lo, hi, clo), jnp.int32(0)))
    need = kf - chi
    ri = lax.broadcasted_iota(jnp.int32, (LANES, LANES), 0)
    ci = lax.broadcasted_iota(jnp.int32, (LANES, LANES), 1)
    tri = jnp.where(ri <= ci, 1.0, 0.0).astype(BF16)
    seen = jnp.zeros_like(need)
    j = jnp.full(lo.shape, NO_INDEX_LIMIT, jnp.int32)
    for t in range(sm.shape[1] // LANES):
        cols = slice(t * LANES, (t + 1) * LANES)
        band = jnp.logical_and(sm[:, cols] >= lo, sm[:, cols] < hi)
        cum = _dot(jnp.where(band, 1.0, 0.0).astype(BF16), tri) + seen
        over = jnp.logical_and(band, cum > need)
        j = jnp.minimum(j, jnp.min(jnp.where(over, kpos[:, cols], NO_INDEX_LIMIT), axis=-1, keepdims=True))
        seen = cum[:, LANES - 1:LANES]
    return lo, hi, jnp.where(clo > kf, j, jnp.int32(NO_INDEX_LIMIT))


ST_LO, ST_HI, ST_CLO, ST_CHI = range(4)


def _dsa_scores(wi_ref, qih_ref, kih_ref, sm_ref, st_ref, *, i, topk, nkeys):
    lane = lax.broadcasted_iota(jnp.int32, (Q_BLOCK, LANES), 1)
    sm_ref[:, 0:nkeys] = jnp.zeros((Q_BLOCK, nkeys), F32)

    def idx_head(h, carry):
        s = _dot(qih_ref[h], kih_ref[0:nkeys, :], _NT)
        w = jnp.sum(jnp.where(lane == IDX_DIM + h, wi_ref[...], 0.0), axis=-1, keepdims=True)
        sm_ref[:, 0:nkeys] += w * jnp.maximum(s, 0.0)
        return carry

    lax.fori_loop(0, IDX_HEADS, idx_head, 0)

    qpos = i * Q_BLOCK + lax.broadcasted_iota(jnp.int32, (Q_BLOCK, 1), 0)
    kpos = lax.broadcasted_iota(jnp.int32, (1, nkeys), 1)
    adm = _chunk_id(kpos) <= _chunk_id(qpos)
    sm = jnp.where(adm, sm_ref[:, 0:nkeys], -jnp.inf)
    sm_ref[:, 0:nkeys] = sm
    st = _select_fixed(sm, _count(adm), topk)
    for c, val in enumerate(st):
        st_ref[:, c:c + 1] = val


def _dsa_attend(o_ref, qh_ref, kh_ref, vh_ref, sm_ref, st_ref, j_ref, oh_ref, *, nkeys):
    kpos = lax.broadcasted_iota(jnp.int32, (1, nkeys), 1)
    sm = sm_ref[:, 0:nkeys]
    lo, hi = st_ref[:, ST_LO:ST_LO + 1], st_ref[:, ST_HI:ST_HI + 1]
    sel = jnp.logical_or(sm >= hi, jnp.logical_and(sm >= lo, kpos < j_ref[:, 0:1]))
    sm_ref[:, 0:nkeys] = jnp.where(sel, 0.0, -jnp.inf)
    group = A_HEADS // A_KV_HEADS

    def att_heads(t, carry):
        hs = [t * ATT_HEADS_PER_TRIP + d for d in range(ATT_HEADS_PER_TRIP)]
        ns = [h // group for h in hs]
        s = [_dot(qh_ref[h], kh_ref[n, 0:nkeys, :], _NT) + sm_ref[:, 0:nkeys] for h, n in zip(hs, ns)]
        m = [jnp.max(x, axis=-1, keepdims=True) for x in s]
        p = [jnp.exp(x - mx) for x, mx in zip(s, m)]
        l = [jnp.sum(x, axis=-1, keepdims=True) for x in p]
        o = [_dot(x.astype(BF16), vh_ref[n, 0:nkeys, :]) / lx for x, n, lx in zip(p, ns, l)]
        oh_ref[pl.ds(hs[0], ATT_HEADS_PER_TRIP)] = jnp.stack(o)
        return carry

    lax.fori_loop(0, A_HEADS // ATT_HEADS_PER_TRIP, att_heads, 0)
    o_ref[...] = jnp.concatenate([oh_ref[h] for h in range(A_HEADS)], axis=-1)


def _key_extents(nq):
    return sorted({min(nq, max(1, round(nq * f))) for f in KEY_EXTENT_FRACTIONS} | {nq})


def _dsa_kernel(q_ref, qi_ref, wi_ref, k_ref, v_ref, ki_ref, o_ref,
                qh_ref, qih_ref, kh_ref, vh_ref, kih_ref, sm_ref, st_ref, j_ref, oh_ref, *, topk, nq):
    i = pl.program_id(1)
    lp = nq * Q_BLOCK

    @pl.when(i == 0)
    def _():
        for n in range(A_KV_HEADS):
            kh_ref[n] = k_ref[:, n * A_HEAD_DIM:(n + 1) * A_HEAD_DIM]
            vh_ref[n] = v_ref[:, n * A_HEAD_DIM:(n + 1) * A_HEAD_DIM]
        kih_ref[...] = ki_ref[:, 0:IDX_DIM]

    scale = jnp.asarray(A_HEAD_DIM ** -0.5, BF16)
    for h in range(A_HEADS):
        qh_ref[h] = q_ref[:, h * A_HEAD_DIM:(h + 1) * A_HEAD_DIM] * scale
    for h in range(IDX_HEADS):
        qih_ref[h] = qi_ref[:, h * IDX_DIM:(h + 1) * IDX_DIM]

    last_chunk = 1 + ((i + 1) * Q_BLOCK - 1 - N_META) // CHUNK
    keys_end = N_META + CHUNK * last_chunk
    need = jnp.minimum((keys_end + Q_BLOCK - 1) // Q_BLOCK, nq)
    extents = _key_extents(nq)
    lows = [0] + extents[:-1]
    in_class = [jnp.logical_and(need > a, need <= e) for a, e in zip(lows, extents)]

    for e, on in zip(extents, in_class):
        @pl.when(on)
        def _(e=e):
            _dsa_scores(wi_ref, qih_ref, kih_ref, sm_ref, st_ref, i=i, topk=topk, nkeys=e * Q_BLOCK)

    j_ref[...] = jnp.full(j_ref.shape, NO_INDEX_LIMIT, jnp.int32)

    @pl.when(_any(st_ref[:, ST_CLO:ST_CLO + 1] > float(topk)))
    def _():
        nkeys = Q_BLOCK * sum(jnp.where(on, e, 0) for e, on in zip(extents, in_class))
        kpos = lax.broadcasted_iota(jnp.int32, (1, lp), 1)
        sm = jnp.where(kpos < nkeys, sm_ref[...], -jnp.inf)
        st = tuple(st_ref[:, c:c + 1] for c in range(4))
        lo, hi, j = _select_refine(sm, kpos, topk, st)
        st_ref[:, ST_LO:ST_LO + 1] = lo
        st_ref[:, ST_HI:ST_HI + 1] = hi
        j_ref[:, 0:1] = j

    for e, on in zip(extents, in_class):
        @pl.when(on)
        def _(e=e):
            _dsa_attend(o_ref, qh_ref, kh_ref, vh_ref, sm_ref, st_ref, j_ref, oh_ref, nkeys=e * Q_BLOCK)


def _dsa_call(ua, kiw, B, lp, topk):
    nq = lp // Q_BLOCK
    kern = functools.partial(_dsa_kernel, topk=topk, nq=nq)
    qrow = lambda b, i: b * nq + i
    return pl.pallas_call(
        kern,
        grid=(B, nq),
        in_specs=[pl.BlockSpec((Q_BLOCK, A_Q), lambda b, i: (qrow(b, i), 0)),
                  pl.BlockSpec((Q_BLOCK, IDX_Q), lambda b, i: (qrow(b, i), A_Q // IDX_Q)),
                  pl.BlockSpec((Q_BLOCK, LANES), lambda b, i: (qrow(b, i), 0)),
                  pl.BlockSpec((lp, A_KV), lambda b, i: (b, (A_Q + IDX_Q) // A_KV)),
                  pl.BlockSpec((lp, A_KV), lambda b, i: (b, (A_Q + IDX_Q) // A_KV + 1)),
                  pl.BlockSpec((lp, LANES), lambda b, i: (b, KIW_BLOCK))],
        out_specs=pl.BlockSpec((Q_BLOCK, A_Q), lambda b, i: (qrow(b, i), 0)),
        out_shape=jax.ShapeDtypeStruct((B * lp, A_Q), F32),
        scratch_shapes=[pltpu.VMEM((A_HEADS, Q_BLOCK, A_HEAD_DIM), BF16),
                        pltpu.VMEM((IDX_HEADS, Q_BLOCK, IDX_DIM), BF16),
                        pltpu.VMEM((A_KV_HEADS, lp, A_HEAD_DIM), BF16),
                        pltpu.VMEM((A_KV_HEADS, lp, A_HEAD_DIM), BF16),
                        pltpu.VMEM((lp, IDX_DIM), BF16),
                        pltpu.VMEM((Q_BLOCK, lp), F32),
                        pltpu.VMEM((Q_BLOCK, LANES), F32),
                        pltpu.VMEM((Q_BLOCK, LANES), jnp.int32),
                        pltpu.VMEM((A_HEADS, Q_BLOCK, A_HEAD_DIM), F32)],
        compiler_params=_params(("parallel", "arbitrary")),
        name="dsa",
    )(ua, ua, kiw, ua, ua, ua)


def _rwkv_a_kernel(ur_ref, mu_ref, w0_ref, wup_ref, a0_ref, aup_ref, gup_ref, kk_ref, ka_ref, rk_ref,
                   tril_ref, hsum_ref,
                   at_ref, bt_ref, kt_ref, rt_ref, v_ref, gc_ref, bonus_ref, g_ref, prev_ref):
    t = pl.program_id(1)
    tm = ur_ref.shape[0]

    @pl.when(t == 0)
    def _():
        prev_ref[...] = jnp.zeros_like(prev_ref)

    u = ur_ref[...]
    rows = lax.broadcasted_iota(jnp.int32, (tm, 1), 0)
    u_prev = jnp.where(rows == 0, prev_ref[0:1, :], pltpu.roll(u, 1, 0))
    prev_ref[0:1, :] = u[tm - 1:tm, :]
    u = u + (u_prev - u) * mu_ref[...]

    W = R_WIDTH
    r, k, v = u[:, 0:W], u[:, W:2 * W], u[:, 2 * W:3 * W]
    xw = u[:, 3 * W:3 * W + W_LORA]
    xa = u[:, 3 * W + W_LORA:3 * W + W_LORA + A_LORA]
    xg = u[:, 3 * W + W_LORA + A_LORA:]

    hp = lax.Precision.HIGHEST
    zw = w0_ref[...] + _dot(jnp.tanh(xw), wup_ref[...], precision=hp)
    logw = -(jnp.maximum(-zw, 0.0) + jnp.log1p(jnp.exp(-jnp.abs(zw)))) - 0.5
    lw = -jnp.exp(logw)
    a = jax.nn.sigmoid(a0_ref[...] + _dot(xa, aup_ref[...], precision=hp))
    g = _dot(jax.nn.sigmoid(xg), gup_ref[...], precision=hp)

    hsum = hsum_ref[...]
    kk = k * kk_ref[...]
    kk = kk * lax.rsqrt(jnp.maximum(_dot_split(kk * kk, hsum), 1e-24))
    kmod = k * (1.0 + (a - 1.0) * ka_ref[...])
    bonus_ref[...] = _dot_split(r * kmod * rk_ref[...], hsum) * v
    g_ref[...] = g
    v_ref[...] = v.astype(v_ref.dtype)

    cs = _dot(tril_ref[...], lw, precision=hp)
    e_neg = jnp.exp(-cs)
    at_ref[...] = (-kk * jnp.exp(cs - lw)).astype(at_ref.dtype)
    bt_ref[...] = (kk * a * e_neg).astype(bt_ref.dtype)
    kt_ref[...] = (kmod * e_neg).astype(kt_ref.dtype)
    rt_ref[...] = (r * jnp.exp(cs)).astype(rt_ref.dtype)
    for c in range(tm // RW_CHUNK):
        gc_ref[c] = jnp.exp(cs[(c + 1) * RW_CHUNK - 1:(c + 1) * RW_CHUNK, :])


def _rwkv_a_call(ur, B, lp, mu, w0, w_up, a0, a_up, g_up, k_k, k_a, r_k):
    T = B * lp
    tm = next(t for t in RWKV_A_TILES if lp % t == 0)
    nt = lp // tm
    W = R_WIDTH
    ri = lax.broadcasted_iota(jnp.int32, (tm, tm), 0)
    ci = lax.broadcasted_iota(jnp.int32, (tm, tm), 1)
    tril = jnp.where(jnp.logical_and(ri >= ci, ri // RW_CHUNK == ci // RW_CHUNK), 1.0, 0.0).astype(F32)
    hr = lax.broadcasted_iota(jnp.int32, (W, W), 0) // R_HEAD_DIM
    hc = lax.broadcasted_iota(jnp.int32, (W, W), 1) // R_HEAD_DIM
    hsum = jnp.where(hr == hc, 1.0, 0.0).astype(BF16)
    vec = lambda x: x.reshape(1, -1)
    const = lambda shape: pl.BlockSpec(shape, lambda b, t: (0,) * len(shape))
    rowspec = lambda n: pl.BlockSpec((tm, n), lambda b, t: (b * nt + t, 0))
    nck = tm // RW_CHUNK
    out_rows = jax.ShapeDtypeStruct((T, W), F32)
    mxu_rows = jax.ShapeDtypeStruct((T, W), BF16)
    return pl.pallas_call(
        _rwkv_a_kernel,
        grid=(B, nt),
        in_specs=[rowspec(R_COLS), const((1, R_COLS)), const((1, W)), const((W_LORA, W)), const((1, W)),
                  const((A_LORA, W)), const((G_LORA, W)), const((1, W)), const((1, W)), const((1, W)),
                  const((tm, tm)), const((W, W))],
        out_specs=[rowspec(W), rowspec(W), rowspec(W), rowspec(W), rowspec(W),
                   pl.BlockSpec((nck, 1, W), lambda b, t: (b * nt + t, 0, 0)),
                   rowspec(W), rowspec(W)],
        out_shape=[mxu_rows, mxu_rows, mxu_rows, mxu_rows, mxu_rows,
                   jax.ShapeDtypeStruct((T // RW_CHUNK, 1, W), F32), out_rows, out_rows],
        scratch_shapes=[pltpu.VMEM((8, R_COLS), F32)],
        compiler_params=_params(("parallel", "arbitrary")),
        name="rwkv_a",
    )(ur, vec(mu), vec(w0), w_up, vec(a0), a_up, g_up, vec(k_k), vec(k_a), vec(r_k), tril, hsum)


def _rwkv_b_kernel(at_ref, bt_ref, kt_ref, rt_ref, v_ref, gc_ref, y_ref, s_ref):
    c = pl.program_id(1)

    @pl.when(c == 0)
    def _():
        s_ref[...] = jnp.zeros_like(s_ref)

    C = RW_CHUNK
    P = 2 * C
    lane = lax.broadcasted_iota(jnp.int32, (C, P), 1)
    lo = lane < R_HEAD_DIM
    ri = lax.broadcasted_iota(jnp.int32, (P, P), 0)
    ci = lax.broadcasted_iota(jnp.int32, (P, P), 1)
    same = (ri // C) == (ci // C)
    strict = jnp.logical_and(same, ri > ci)
    incl = jnp.logical_and(same, ri >= ci)
    eye = jnp.where(ri == ci, 1.0, 0.0).astype(F32)
    mm = _dot_bf16
    zero = jnp.zeros((), at_ref.dtype)

    def stack(x):
        return jnp.concatenate([jnp.where(lo, x, zero), jnp.where(lo, zero, x)], axis=0)

    pairs = range(R_HEADS // 2)
    sls = [slice(p * P, (p + 1) * P) for p in pairs]
    a_s = [stack(at_ref[:, sl]) for sl in sls]
    r_s = [stack(rt_ref[:, sl]) for sl in sls]
    v_s = [stack(v_ref[:, sl]) for sl in sls]
    bk_s = [jnp.concatenate([stack(bt_ref[:, sl]), stack(kt_ref[:, sl])], axis=0) for sl in sls]
    s0 = [s_ref[p] for p in pairs]
    g = [mm(jnp.concatenate([a_s[p], r_s[p]], axis=0), bk_s[p], _NT) for p in pairs]
    l_ab = [jnp.where(strict, g[p][0:P, 0:P], 0.0) for p in pairs]
    l_ak = [jnp.where(strict, g[p][0:P, P:2 * P], 0.0) for p in pairs]
    m_r = [jnp.where(jnp.concatenate([incl, incl], axis=1), g[p][P:2 * P, :], 0.0) for p in pairs]
    t_inv = [eye + l_ab[p] for p in pairs]
    pw = l_ab
    for _ in range(C.bit_length() - 2):
        pw = [mm(pw[p], pw[p]) for p in pairs]
        t_inv = [t_inv[p] + mm(t_inv[p], pw[p]) for p in pairs]
    z = [mm(a_s[p], s0[p], _NT) + mm(l_ak[p], v_s[p]) for p in pairs]
    uv = [jnp.concatenate([mm(t_inv[p], z[p]).astype(at_ref.dtype), v_s[p]], axis=0) for p in pairs]
    y_s = [mm(r_s[p], s0[p], _NT) + mm(m_r[p], uv[p]) for p in pairs]
    s_new = [(s0[p] + mm(uv[p], bk_s[p], _TN)) * gc_ref[0, :, sls[p]] for p in pairs]
    y_ref[...] = jnp.concatenate([y[0:C, :] + y[C:P, :] for y in y_s], axis=-1)
    s_ref[...] = jnp.stack(s_new, axis=0)


def _rwkv_b_call(at, bt, kt, rt, v, gc, B, lp):
    T, W = at.shape
    nc = lp // RW_CHUNK
    rowspec = pl.BlockSpec((RW_CHUNK, W), lambda b, c: (b * nc + c, 0))
    return pl.pallas_call(
        _rwkv_b_kernel,
        grid=(B, nc),
        in_specs=[rowspec, rowspec, rowspec, rowspec, rowspec,
                  pl.BlockSpec((1, 1, W), lambda b, c: (b * nc + c, 0, 0))],
        out_specs=rowspec,
        out_shape=jax.ShapeDtypeStruct((T, W), F32),
        scratch_shapes=[pltpu.VMEM((R_HEADS // 2, 2 * RW_CHUNK, 2 * R_HEAD_DIM), F32)],
        compiler_params=_params(("parallel", "arbitrary")),
        name="rwkv_b",
    )(at, bt, kt, rt, v, gc)


def _merge_kernel(h_ref, oa_ref, y_ref, bonus_ref, g_ref, ug_ref, bg_ref, gng_ref, gnb_ref, hmean_ref,
                  wa_ref, wb_ref, wo_ref, lng_ref, lnb_ref, o_ref, *, alpha):
    y = y_ref[...]
    hmean = hmean_ref[...]
    m = _dot_split(y, hmean)
    yc = y - m
    var = _dot_split(yc * yc, hmean)
    yn = yc * lax.rsqrt(var + GN_EPS) * gng_ref[...] + gnb_ref[...]
    ob = (yn + bonus_ref[...]) * g_ref[...]
    gates = jax.nn.sigmoid(ug_ref[...] + bg_ref[...])
    merged = (gates[:, 0:D_MODEL] * _dot_bf16(oa_ref[...], wa_ref[...])
              + gates[:, D_MODEL:] * _dot_bf16(ob, wb_ref[...]))
    mix = _dot_bf16(merged, wo_ref[...])
    o_ref[...] = _layer_norm(alpha * h_ref[...] + mix, lng_ref[...], lnb_ref[...])


def _merge_call(h, oa, y, bonus, g, ug, b_gate, gn_g, gn_b, wa, wb, wo, ln_g, ln_b, alpha):
    T, D = h.shape
    W = R_WIDTH
    hr = lax.broadcasted_iota(jnp.int32, (W, W), 0) // R_HEAD_DIM
    hc = lax.broadcasted_iota(jnp.int32, (W, W), 1) // R_HEAD_DIM
    hmean = jnp.where(hr == hc, 1.0 / R_HEAD_DIM, 0.0).astype(BF16)
    vec = lambda x: x.reshape(1, -1)
    row = lambda n: pl.BlockSpec((ROW_TILE, n), lambda i: (i, 0))
    const = lambda a, b: pl.BlockSpec((a, b), lambda i: (0, 0))
    return pl.pallas_call(
        functools.partial(_merge_kernel, alpha=alpha),
        grid=(T // ROW_TILE,),
        in_specs=[row(D), row(A_Q), row(W), row(W), row(W), row(GATE_COLS), const(1, GATE_COLS),
                  const(1, W), const(1, W), const(W, W), const(A_Q, D), const(W, D), const(D, D),
                  const(1, D), const(1, D)],
        out_specs=row(D),
        out_shape=jax.ShapeDtypeStruct((T, D), F32),
        compiler_params=_params(("parallel",)),
        name="merge",
    )(h, oa, y, bonus, g, ug, vec(b_gate), vec(gn_g), vec(gn_b), hmean,
      wa.astype(BF16), wb.astype(BF16), wo.astype(BF16), vec(ln_g), vec(ln_b))


MOE_TILES = (1536, 1024, 768, 512, 256)


def _route(logits_t, bias_t):
    s = jax.nn.sigmoid(logits_t)
    sel = s + bias_t
    row = lambda a, e: a[e:e + 1, :]
    gscore = []
    for gi in range(N_GROUPS):
        a, b, c, d = (row(sel, gi * EXPERTS_PER_GROUP + j) for j in range(EXPERTS_PER_GROUP))
        hi1, lo1 = jnp.maximum(a, b), jnp.minimum(a, b)
        hi2, lo2 = jnp.maximum(c, d), jnp.minimum(c, d)
        top1 = jnp.maximum(hi1, hi2)
        top2 = jnp.maximum(jnp.minimum(hi1, hi2), jnp.maximum(lo1, lo2))
        gscore.append(top1 + top2)
    chosen = []
    for gi in range(N_GROUPS):
        ok = None
        for gj in range(N_GROUPS):
            if gj == gi:
                continue
            t = gscore[gi] > gscore[gj] if gj < gi else gscore[gi] >= gscore[gj]
            ok = t if ok is None else jnp.logical_and(ok, t)
        chosen.append(ok)
    picked = []
    for e in range(N_EXPERTS):
        gi = e // EXPERTS_PER_GROUP
        rank = jnp.zeros_like(row(sel, e))
        for e2 in range(gi * EXPERTS_PER_GROUP, (gi + 1) * EXPERTS_PER_GROUP):
            if e2 == e:
                continue
            ahead = row(sel, e2) > row(sel, e) if e2 > e else row(sel, e2) >= row(sel, e)
            rank = rank + jnp.where(ahead, 1.0, 0.0)
        take = jnp.logical_and(chosen[gi], rank < 2.0)
        picked.append(jnp.where(take, row(s, e), 0.0))
    total = picked[0]
    for e in range(1, N_EXPERTS):
        total = total + picked[e]
    return jnp.concatenate(picked, axis=0) / total


def _moe_kernel(h_ref, wr_ref, br_ref, w1_ref, w3_ref, w2_ref, lng_ref, lnb_ref, o_ref,
                xb_ref, comb_ref, acc_ref, *, alpha):
    e = pl.program_id(1)

    @pl.when(e == 0)
    def _():
        x = h_ref[...]
        xb_ref[...] = x.astype(BF16)
        logits_t = _dot(wr_ref[...], x, _NT, precision=lax.Precision.HIGHEST)
        comb_t = _route(logits_t, br_ref[...])
        pad = jnp.zeros((LANES - N_EXPERTS, comb_t.shape[1]), F32)
        comb_ref[...] = jnp.transpose(jnp.concatenate([comb_t, pad], axis=0))
        acc_ref[...] = jnp.zeros_like(acc_ref)

    xb = xb_ref[...]
    h1 = _dot(xb, w1_ref[0])
    h3 = _dot(xb, w3_ref[0])
    act = (h1 * jax.nn.sigmoid(h1)) * h3
    he = _dot(act.astype(BF16), w2_ref[0])
    lane = lax.broadcasted_iota(jnp.int32, comb_ref.shape, 1)
    ce = jnp.sum(jnp.where(lane == e, comb_ref[...], 0.0), axis=-1, keepdims=True)
    acc_ref[...] += ce * he

    @pl.when(e == pl.num_programs(1) - 1)
    def _():
        o_ref[...] = _layer_norm(alpha * h_ref[...] + acc_ref[...], lng_ref[...], lnb_ref[...])


def _moe_call(h, w_router_t, b_router, w1, w3, w2, ln_g, ln_b, alpha, tile):
    T, D = h.shape
    vec = lambda x: x.reshape(1, -1)
    return pl.pallas_call(
        functools.partial(_moe_kernel, alpha=alpha),
        grid=(T // tile, N_EXPERTS),
        in_specs=[pl.BlockSpec((tile, D), lambda i, e: (i, 0), pipeline_mode=pl.Buffered(1)),
                  pl.BlockSpec((N_EXPERTS, D), lambda i, e: (0, 0)),
                  pl.BlockSpec((N_EXPERTS, 1), lambda i, e: (0, 0)),
                  pl.BlockSpec((1, D, D_EXPERT), lambda i, e: (e, 0, 0)),
                  pl.BlockSpec((1, D, D_EXPERT), lambda i, e: (e, 0, 0)),
                  pl.BlockSpec((1, D_EXPERT, D), lambda i, e: (e, 0, 0)),
                  pl.BlockSpec((1, D), lambda i, e: (0, 0)),
                  pl.BlockSpec((1, D), lambda i, e: (0, 0))],
        out_specs=pl.BlockSpec((tile, D), lambda i, e: (i, 0)),
        out_shape=jax.ShapeDtypeStruct((T, D), F32),
        scratch_shapes=[pltpu.VMEM((tile, D), BF16), pltpu.VMEM((tile, LANES), F32),
                        pltpu.VMEM((tile, D), F32)],
        compiler_params=_params(("parallel", "arbitrary")),
        name="moe",
    )(h, w_router_t, b_router.reshape(N_EXPERTS, 1), w1, w3, w2, vec(ln_g), vec(ln_b))


def _split_w_in(w):
    o = 0
    q = w[:, o:o + A_Q]; o += A_Q
    k = w[:, o:o + A_KV]; o += A_KV
    v = w[:, o:o + A_KV]; o += A_KV
    qi = w[:, o:o + IDX_Q]; o += IDX_Q
    kiw = w[:, o:o + IDX_DIM + IDX_HEADS]; o += IDX_DIM + IDX_HEADS
    pad = jnp.zeros((w.shape[0], LANES - IDX_DIM - IDX_HEADS), w.dtype)
    wa = jnp.concatenate([q, qi, k, v, kiw, pad], axis=1)
    wr = w[:, o:o + R_COLS]; o += R_COLS
    wg = w[:, o:o + GATE_COLS]
    return wa.astype(BF16), wr.astype(BF16), wg.astype(BF16)


def _padded_len(L):
    return -(-L // Q_BLOCK) * Q_BLOCK


def kernel(x, meta_tokens, ln_in_g, ln_in_b, w_in, b_gate, rwkv_mu, rwkv_w0, rwkv_w_up, rwkv_a0, rwkv_a_up,
           rwkv_g_up, rwkv_k_k, rwkv_k_a, rwkv_r_k, rwkv_gn_g, rwkv_gn_b, w_branch_a, w_branch_b, w_out,
           ln1_g, ln1_b, ln2_g, ln2_b, w_router, b_router, w_exp1, w_exp3, w_exp2):
    B, S, D = x.shape
    depth = w_in.shape[0]
    alpha = (2 * depth) ** 0.25
    topk = min(TOPK_MAX, S // 4)
    L = N_META + S
    lp = _padded_len(L)
    T = B * lp
    assert T % ROW_TILE == 0
    moe_tile = next(t for t in MOE_TILES if T % t == 0)
    meta = jnp.broadcast_to(meta_tokens[None].astype(x.dtype), (B, N_META, D))
    hin = jnp.concatenate([meta, x, jnp.zeros((B, lp - L, D), x.dtype)], axis=1).reshape(T, D)
    h = _ln_call(hin, ln_in_g, ln_in_b)
    w_router_t = jnp.transpose(w_router)
    for l in range(depth):
        wa, wr, wg = _split_w_in(w_in[l])
        ua, kiw, ur, ug = _proj_call(h, wa, wr, wg)
        oa = _dsa_call(ua, kiw, B, lp, topk)
        at, bt, kt, rt, v, gc, bonus, g = _rwkv_a_call(
            ur, B, lp, rwkv_mu[l], rwkv_w0[l], rwkv_w_up[l], rwkv_a0[l], rwkv_a_up[l], rwkv_g_up[l],
            rwkv_k_k[l], rwkv_k_a[l], rwkv_r_k[l])
        y = _rwkv_b_call(at, bt, kt, rt, v, gc, B, lp)
        h = _merge_call(h, oa, y, bonus, g, ug, b_gate[l], rwkv_gn_g[l], rwkv_gn_b[l],
                        w_branch_a[l], w_branch_b[l], w_out[l], ln1_g[l], ln1_b[l], alpha)
        h = _moe_call(h, w_router_t, b_router, w_exp1[l].astype(BF16), w_exp3[l].astype(BF16),
                      w_exp2[l].astype(BF16), ln2_g[l], ln2_b[l], alpha, moe_tile)
    return h.reshape(B, lp, D)[:, N_META:L]
```

```python
import functools

import jax
import jax.numpy as jnp
from jax import lax
from jax.experimental import pallas as pl
from jax.experimental.pallas import tpu as pltpu

D_MODEL = 1024
CHUNK = 64
N_META = 16
Q_BLOCK = 128
A_HEADS = 8
A_KV_HEADS = 2
A_HEAD_DIM = 64
IDX_HEADS = 8
IDX_DIM = 32
TOPK_MAX = 256
A_Q = A_HEADS * A_HEAD_DIM
A_KV = A_KV_HEADS * A_HEAD_DIM
IDX_Q = IDX_HEADS * IDX_DIM
R_HEADS = 8
R_HEAD_DIM = 64
R_WIDTH = R_HEADS * R_HEAD_DIM
W_LORA = 64
A_LORA = 64
G_LORA = 128
R_COLS = 3 * R_WIDTH + W_LORA + A_LORA + G_LORA
GN_EPS = 64e-5
GATE_COLS = 2 * D_MODEL
N_EXPERTS = 16
N_GROUPS = 4
EXPERTS_PER_GROUP = N_EXPERTS // N_GROUPS
D_EXPERT = 512
LN_EPS = 1e-5

LANES = 128
UA_COLS = A_Q + IDX_Q + 2 * A_KV + LANES
KIW_BLOCK = (A_Q + IDX_Q + 2 * A_KV) // LANES
ROW_TILE = 256
RWKV_A_TILES = (384, 256, 128)
RW_CHUNK = 64
VMEM_LIMIT = 56 * 1024 * 1024

F32 = jnp.float32
BF16 = jnp.bfloat16
INT_MIN = -2147483648


def _dot(a, b, dims=(((1,), (0,)), ((), ())), precision=None):
    return lax.dot_general(a, b, dims, precision=precision, preferred_element_type=F32)


def _dot_bf16(a, b, dims=(((1,), (0,)), ((), ()))):
    return _dot(a.astype(BF16), b.astype(BF16), dims)


_NT = (((1,), (1,)), ((), ()))
_TN = (((0,), (0,)), ((), ()))


def _dot_split(x, m_bf16):
    hi = x.astype(BF16)
    lo = (x - hi.astype(F32)).astype(BF16)
    return _dot(hi, m_bf16) + _dot(lo, m_bf16)


def _layer_norm(x, g, b):
    mu = jnp.mean(x, axis=-1, keepdims=True)
    xc = x - mu
    var = jnp.mean(xc * xc, axis=-1, keepdims=True)
    return xc * lax.rsqrt(var + LN_EPS) * g + b


def _params(sem):
    return pltpu.CompilerParams(dimension_semantics=sem, vmem_limit_bytes=VMEM_LIMIT)


def _ln_kernel(x_ref, g_ref, b_ref, o_ref):
    o_ref[...] = _layer_norm(x_ref[...], g_ref[...], b_ref[...])


def _ln_call(x, g, b):
    T, D = x.shape
    return pl.pallas_call(
        _ln_kernel,
        grid=(T // ROW_TILE,),
        in_specs=[pl.BlockSpec((ROW_TILE, D), lambda i: (i, 0)),
                  pl.BlockSpec((1, D), lambda i: (0, 0)),
                  pl.BlockSpec((1, D), lambda i: (0, 0))],
        out_specs=pl.BlockSpec((ROW_TILE, D), lambda i: (i, 0)),
        out_shape=jax.ShapeDtypeStruct((T, D), F32),
        compiler_params=_params(("parallel",)),
        name="ln_in",
    )(x, g.reshape(1, D), b.reshape(1, D))


def _proj_kernel(h_ref, wa_ref, wr_ref, wg_ref, ua_ref, kiw_ref, ur_ref, ug_ref):
    hb = h_ref[...].astype(BF16)
    ua = _dot(hb, wa_ref[...])
    ua_ref[...] = ua.astype(BF16)
    kiw_ref[...] = ua[:, UA_COLS - LANES:]
    ur_ref[...] = _dot(hb, wr_ref[...])
    ug_ref[...] = _dot(hb, wg_ref[...])


def _proj_call(h, wa, wr, wg):
    T, D = h.shape
    row = lambda n: pl.BlockSpec((ROW_TILE, n), lambda i: (i, 0))
    full = lambda n: pl.BlockSpec((D, n), lambda i: (0, 0))
    return pl.pallas_call(
        _proj_kernel,
        grid=(T // ROW_TILE,),
        in_specs=[row(D), full(UA_COLS), full(R_COLS), full(GATE_COLS)],
        out_specs=[row(UA_COLS), row(LANES), row(R_COLS), row(GATE_COLS)],
        out_shape=[jax.ShapeDtypeStruct((T, UA_COLS), BF16),
                   jax.ShapeDtypeStruct((T, LANES), F32),
                   jax.ShapeDtypeStruct((T, R_COLS), F32),
                   jax.ShapeDtypeStruct((T, GATE_COLS), F32)],
        compiler_params=_params(("parallel",)),
        name="proj",
    )(h, wa, wr, wg)


assert CHUNK & (CHUNK - 1) == 0
CHUNK_SHIFT = CHUNK.bit_length() - 1
BISECT_STEPS = 22
BISECT_ROW_GROUPS = 2
ATT_HEADS_PER_TRIP = 2
REFINE_CAP = 256
KEY_EXTENT_FRACTIONS = (0.18, 0.36, 0.64, 1.0)


def _chunk_id(p):
    return jnp.where(p < N_META, 0, 1 + ((p - N_META) >> CHUNK_SHIFT))


def _count(mask):
    return jnp.sum(jnp.where(mask, 1.0, 0.0), axis=-1, keepdims=True)


def _any(mask):
    return jnp.max(jnp.where(mask, 1.0, 0.0)) > 0.5


NO_INDEX_LIMIT = 2 ** 30


def _bisect_step(sm, kf, st):
    lo, hi, clo, chi = st
    mid = 0.5 * (lo + hi)
    cnt = _count(sm >= mid)
    ge = cnt >= kf
    return (jnp.where(ge, mid, lo), jnp.where(ge, hi, mid), jnp.where(ge, cnt, clo), jnp.where(ge, chi, cnt))


def _select_fixed(sm, n_adm, topk):
    kf = float(topk)
    mx = jnp.max(sm, axis=-1, keepdims=True)
    mn = jnp.min(jnp.where(sm == -jnp.inf, jnp.inf, sm), axis=-1, keepdims=True)
    hi0 = jnp.where(mx >= 0.0, 2.0 * mx + 1.0, 0.5 * mx + 1.0)
    st0 = (mn, hi0, n_adm, jnp.zeros_like(mn))
    rows = sm.shape[0] // BISECT_ROW_GROUPS
    parts = [slice(g * rows, (g + 1) * rows) for g in range(BISECT_ROW_GROUPS)]

    def body(_, sts):
        return tuple(_bisect_step(sm[p], kf, st) for p, st in zip(parts, sts))

    sts = lax.fori_loop(0, BISECT_STEPS, body, tuple(tuple(x[p] for x in st0) for p in parts), unroll=2)
    return tuple(jnp.concatenate([st[c] for st in sts], axis=0) for c in range(4))


def _select_refine(sm, kpos, topk, st):
    kf = float(topk)

    def open_rows(lo, hi, clo):
        band = jnp.logical_and(sm >= lo, sm < hi)
        bmax = jnp.max(jnp.where(band, sm, -jnp.inf), axis=-1, keepdims=True)
        bmin = jnp.min(jnp.where(band, sm, jnp.inf), axis=-1, keepdims=True)
        return _any(jnp.logical_and(clo > kf, bmax != bmin)).astype(jnp.int32)

    def body(c):
        lo, hi, clo, chi = _bisect_step(sm, kf, c[0:4])
        return lo, hi, clo, chi, open_rows(lo, hi, clo), c[5] + 1

    lo, hi, clo, chi = st
    lo, hi, clo, chi, _, _ = lax.while_loop(
        lambda c: jnp.logical_and(c[4] > 0, c[5] < REFINE_CAP), body,
        (lo, hi, clo, chi, open_rows(lo, hi, clo), jnp.int32(0)))
    need = kf - chi
    ri = lax.broadcasted_iota(jnp.int32, (LANES, LANES), 0)
    ci = lax.broadcasted_iota(jnp.int32, (LANES, LANES), 1)
    tri = jnp.where(ri <= ci, 1.0, 0.0).astype(BF16)
    seen = jnp.zeros_like(need)
    j = jnp.full(lo.shape, NO_INDEX_LIMIT, jnp.int32)
    for t in range(sm.shape[1] // LANES):
        cols = slice(t * LANES, (t + 1) * LANES)
        band = jnp.logical_and(sm[:, cols] >= lo, sm[:, cols] < hi)
        cum = _dot(jnp.where(band, 1.0, 0.0).astype(BF16), tri) + seen
        over = jnp.logical_and(band, cum > need)
        j = jnp.minimum(j, jnp.min(jnp.where(over, kpos[:, cols], NO_INDEX_LIMIT), axis=-1, keepdims=True))
        seen = cum[:, LANES - 1:LANES]
    return lo, hi, jnp.where(clo > kf, j, jnp.int32(NO_INDEX_LIMIT))


ST_LO, ST_HI, ST_CLO, ST_CHI = range(4)


def _dsa_scores(wi_ref, qih_ref, kih_ref, sm_ref, st_ref, *, i, topk, nkeys):
    lane = lax.broadcasted_iota(jnp.int32, (Q_BLOCK, LANES), 1)
    sm_ref[:, 0:nkeys] = jnp.zeros((Q_BLOCK, nkeys), F32)

    def idx_head(h, carry):
        s = _dot(qih_ref[h], kih_ref[0:nkeys, :], _NT)
        w = jnp.sum(jnp.where(lane == IDX_DIM + h, wi_ref[...], 0.0), axis=-1, keepdims=True)
        sm_ref[:, 0:nkeys] += w * jnp.maximum(s, 0.0)
        return carry

    lax.fori_loop(0, IDX_HEADS, idx_head, 0)

    qpos = i * Q_BLOCK + lax.broadcasted_iota(jnp.int32, (Q_BLOCK, 1), 0)
    kpos = lax.broadcasted_iota(jnp.int32, (1, nkeys), 1)
    adm = _chunk_id(kpos) <= _chunk_id(qpos)
    sm = jnp.where(adm, sm_ref[:, 0:nkeys], -jnp.inf)
    sm_ref[:, 0:nkeys] = sm
    st = _select_fixed(sm, _count(adm), topk)
    for c, val in enumerate(st):
        st_ref[:, c:c + 1] = val


def _dsa_attend(o_ref, qh_ref, kh_ref, vh_ref, sm_ref, st_ref, j_ref, oh_ref, *, nkeys):
    kpos = lax.broadcasted_iota(jnp.int32, (1, nkeys), 1)
    sm = sm_ref[:, 0:nkeys]
    lo, hi = st_ref[:, ST_LO:ST_LO + 1], st_ref[:, ST_HI:ST_HI + 1]
    sel = jnp.logical_or(sm >= hi, jnp.logical_and(sm >= lo, kpos < j_ref[:, 0:1]))
    sm_ref[:, 0:nkeys] = jnp.where(sel, 0.0, -jnp.inf)
    group = A_HEADS // A_KV_HEADS

    def att_heads(t, carry):
        hs = [t * ATT_HEADS_PER_TRIP + d for d in range(ATT_HEADS_PER_TRIP)]
        ns = [h // group for h in hs]
        s = [_dot(qh_ref[h], kh_ref[n, 0:nkeys, :], _NT) + sm_ref[:, 0:nkeys] for h, n in zip(hs, ns)]
        m = [jnp.max(x, axis=-1, keepdims=True) for x in s]
        p = [jnp.exp(x - mx) for x, mx in zip(s, m)]
        l = [jnp.sum(x, axis=-1, keepdims=True) for x in p]
        o = [_dot(x.astype(BF16), vh_ref[n, 0:nkeys, :]) / lx for x, n, lx in zip(p, ns, l)]
        oh_ref[pl.ds(hs[0], ATT_HEADS_PER_TRIP)] = jnp.stack(o)
        return carry

    lax.fori_loop(0, A_HEADS // ATT_HEADS_PER_TRIP, att_heads, 0)
    o_ref[...] = jnp.concatenate([oh_ref[h] for h in range(A_HEADS)], axis=-1)


def _key_extents(nq):
    return sorted({min(nq, max(1, round(nq * f))) for f in KEY_EXTENT_FRACTIONS} | {nq})


def _dsa_kernel(q_ref, qi_ref, wi_ref, k_ref, v_ref, ki_ref, o_ref,
                qh_ref, qih_ref, kh_ref, vh_ref, kih_ref, sm_ref, st_ref, j_ref, oh_ref, *, topk, nq):
    i = pl.program_id(1)
    lp = nq * Q_BLOCK

    @pl.when(i == 0)
    def _():
        for n in range(A_KV_HEADS):
            kh_ref[n] = k_ref[:, n * A_HEAD_DIM:(n + 1) * A_HEAD_DIM]
            vh_ref[n] = v_ref[:, n * A_HEAD_DIM:(n + 1) * A_HEAD_DIM]
        kih_ref[...] = ki_ref[:, 0:IDX_DIM]

    scale = jnp.asarray(A_HEAD_DIM ** -0.5, BF16)
    for h in range(A_HEADS):
        qh_ref[h] = q_ref[:, h * A_HEAD_DIM:(h + 1) * A_HEAD_DIM] * scale
    for h in range(IDX_HEADS):
        qih_ref[h] = qi_ref[:, h * IDX_DIM:(h + 1) * IDX_DIM]

    last_chunk = 1 + ((i + 1) * Q_BLOCK - 1 - N_META) // CHUNK
    keys_end = N_META + CHUNK * last_chunk
    need = jnp.minimum((keys_end + Q_BLOCK - 1) // Q_BLOCK, nq)
    extents = _key_extents(nq)
    lows = [0] + extents[:-1]
    in_class = [jnp.logical_and(need > a, need <= e) for a, e in zip(lows, extents)]

    for e, on in zip(extents, in_class):
        @pl.when(on)
        def _(e=e):
            _dsa_scores(wi_ref, qih_ref, kih_ref, sm_ref, st_ref, i=i, topk=topk, nkeys=e * Q_BLOCK)

    j_ref[...] = jnp.full(j_ref.shape, NO_INDEX_LIMIT, jnp.int32)

    @pl.when(_any(st_ref[:, ST_CLO:ST_CLO + 1] > float(topk)))
    def _():
        nkeys = Q_BLOCK * sum(jnp.where(on, e, 0) for e, on in zip(extents, in_class))
        kpos = lax.broadcasted_iota(jnp.int32, (1, lp), 1)
        sm = jnp.where(kpos < nkeys, sm_ref[...], -jnp.inf)
        st = tuple(st_ref[:, c:c + 1] for c in range(4))
        lo, hi, j = _select_refine(sm, kpos, topk, st)
        st_ref[:, ST_LO:ST_LO + 1] = lo
        st_ref[:, ST_HI:ST_HI + 1] = hi
        j_ref[:, 0:1] = j

    for e, on in zip(extents, in_class):
        @pl.when(on)
        def _(e=e):
            _dsa_attend(o_ref, qh_ref, kh_ref, vh_ref, sm_ref, st_ref, j_ref, oh_ref, nkeys=e * Q_BLOCK)


def _dsa_call(ua, kiw, B, lp, topk):
    nq = lp // Q_BLOCK
    kern = functools.partial(_dsa_kernel, topk=topk, nq=nq)
    qrow = lambda b, i: b * nq + i
    return pl.pallas_call(
        kern,
        grid=(B, nq),
        in_specs=[pl.BlockSpec((Q_BLOCK, A_Q), lambda b, i: (qrow(b, i), 0)),
                  pl.BlockSpec((Q_BLOCK, IDX_Q), lambda b, i: (qrow(b, i), A_Q // IDX_Q)),
                  pl.BlockSpec((Q_BLOCK, LANES), lambda b, i: (qrow(b, i), 0)),
                  pl.BlockSpec((lp, A_KV), lambda b, i: (b, (A_Q + IDX_Q) // A_KV)),
                  pl.BlockSpec((lp, A_KV), lambda b, i: (b, (A_Q + IDX_Q) // A_KV + 1)),
                  pl.BlockSpec((lp, LANES), lambda b, i: (b, KIW_BLOCK))],
        out_specs=pl.BlockSpec((Q_BLOCK, A_Q), lambda b, i: (qrow(b, i), 0)),
        out_shape=jax.ShapeDtypeStruct((B * lp, A_Q), F32),
        scratch_shapes=[pltpu.VMEM((A_HEADS, Q_BLOCK, A_HEAD_DIM), BF16),
                        pltpu.VMEM((IDX_HEADS, Q_BLOCK, IDX_DIM), BF16),
                        pltpu.VMEM((A_KV_HEADS, lp, A_HEAD_DIM), BF16),
                        pltpu.VMEM((A_KV_HEADS, lp, A_HEAD_DIM), BF16),
                        pltpu.VMEM((lp, IDX_DIM), BF16),
                        pltpu.VMEM((Q_BLOCK, lp), F32),
                        pltpu.VMEM((Q_BLOCK, LANES), F32),
                        pltpu.VMEM((Q_BLOCK, LANES), jnp.int32),
                        pltpu.VMEM((A_HEADS, Q_BLOCK, A_HEAD_DIM), F32)],
        compiler_params=_params(("parallel", "arbitrary")),
        name="dsa",
    )(ua, ua, kiw, ua, ua, ua)


def _rwkv_a_kernel(ur_ref, mu_ref, w0_ref, wup_ref, a0_ref, aup_ref, gup_ref, kk_ref, ka_ref, rk_ref,
                   tril_ref, hsum_ref,
                   at_ref, bt_ref, kt_ref, rt_ref, v_ref, gc_ref, bonus_ref, g_ref, prev_ref):
    t = pl.program_id(1)
    tm = ur_ref.shape[0]

    @pl.when(t == 0)
    def _():
        prev_ref[...] = jnp.zeros_like(prev_ref)

    u = ur_ref[...]
    rows = lax.broadcasted_iota(jnp.int32, (tm, 1), 0)
    u_prev = jnp.where(rows == 0, prev_ref[0:1, :], pltpu.roll(u, 1, 0))
    prev_ref[0:1, :] = u[tm - 1:tm, :]
    u = u + (u_prev - u) * mu_ref[...]

    W = R_WIDTH
    r, k, v = u[:, 0:W], u[:, W:2 * W], u[:, 2 * W:3 * W]
    xw = u[:, 3 * W:3 * W + W_LORA]
    xa = u[:, 3 * W + W_LORA:3 * W + W_LORA + A_LORA]
    xg = u[:, 3 * W + W_LORA + A_LORA:]

    hp = lax.Precision.HIGHEST
    zw = w0_ref[...] + _dot(jnp.tanh(xw), wup_ref[...], precision=hp)
    logw = -(jnp.maximum(-zw, 0.0) + jnp.log1p(jnp.exp(-jnp.abs(zw)))) - 0.5
    lw = -jnp.exp(logw)
    a = jax.nn.sigmoid(a0_ref[...] + _dot(xa, aup_ref[...], precision=hp))
    g = _dot(jax.nn.sigmoid(xg), gup_ref[...], precision=hp)

    hsum = hsum_ref[...]
    kk = k * kk_ref[...]
    kk = kk * lax.rsqrt(jnp.maximum(_dot_split(kk * kk, hsum), 1e-24))
    kmod = k * (1.0 + (a - 1.0) * ka_ref[...])
    bonus_ref[...] = _dot_split(r * kmod * rk_ref[...], hsum) * v
    g_ref[...] = g
    v_ref[...] = v.astype(v_ref.dtype)

    cs = _dot(tril_ref[...], lw, precision=hp)
    e_neg = jnp.exp(-cs)
    at_ref[...] = (-kk * jnp.exp(cs - lw)).astype(at_ref.dtype)
    bt_ref[...] = (kk * a * e_neg).astype(bt_ref.dtype)
    kt_ref[...] = (kmod * e_neg).astype(kt_ref.dtype)
    rt_ref[...] = (r * jnp.exp(cs)).astype(rt_ref.dtype)
    for c in range(tm // RW_CHUNK):
        gc_ref[c] = jnp.exp(cs[(c + 1) * RW_CHUNK - 1:(c + 1) * RW_CHUNK, :])


def _rwkv_a_call(ur, B, lp, mu, w0, w_up, a0, a_up, g_up, k_k, k_a, r_k):
    T = B * lp
    tm = next(t for t in RWKV_A_TILES if lp % t == 0)
    nt = lp // tm
    W = R_WIDTH
    ri = lax.broadcasted_iota(jnp.int32, (tm, tm), 0)
    ci = lax.broadcasted_iota(jnp.int32, (tm, tm), 1)
    tril = jnp.where(jnp.logical_and(ri >= ci, ri // RW_CHUNK == ci // RW_CHUNK), 1.0, 0.0).astype(F32)
    hr = lax.broadcasted_iota(jnp.int32, (W, W), 0) // R_HEAD_DIM
    hc = lax.broadcasted_iota(jnp.int32, (W, W), 1) // R_HEAD_DIM
    hsum = jnp.where(hr == hc, 1.0, 0.0).astype(BF16)
    vec = lambda x: x.reshape(1, -1)
    const = lambda shape: pl.BlockSpec(shape, lambda b, t: (0,) * len(shape))
    rowspec = lambda n: pl.BlockSpec((tm, n), lambda b, t: (b * nt + t, 0))
    nck = tm // RW_CHUNK
    out_rows = jax.ShapeDtypeStruct((T, W), F32)
    mxu_rows = jax.ShapeDtypeStruct((T, W), BF16)
    return pl.pallas_call(
        _rwkv_a_kernel,
        grid=(B, nt),
        in_specs=[rowspec(R_COLS), const((1, R_COLS)), const((1, W)), const((W_LORA, W)), const((1, W)),
                  const((A_LORA, W)), const((G_LORA, W)), const((1, W)), const((1, W)), const((1, W)),
                  const((tm, tm)), const((W, W))],
        out_specs=[rowspec(W), rowspec(W), rowspec(W), rowspec(W), rowspec(W),
                   pl.BlockSpec((nck, 1, W), lambda b, t: (b * nt + t, 0, 0)),
                   rowspec(W), rowspec(W)],
        out_shape=[mxu_rows, mxu_rows, mxu_rows, mxu_rows, mxu_rows,
                   jax.ShapeDtypeStruct((T // RW_CHUNK, 1, W), F32), out_rows, out_rows],
        scratch_shapes=[pltpu.VMEM((8, R_COLS), F32)],
        compiler_params=_params(("parallel", "arbitrary")),
        name="rwkv_a",
    )(ur, vec(mu), vec(w0), w_up, vec(a0), a_up, g_up, vec(k_k), vec(k_a), vec(r_k), tril, hsum)


def _rwkv_b_kernel(at_ref, bt_ref, kt_ref, rt_ref, v_ref, gc_ref, y_ref, s_ref):
    c = pl.program_id(1)

    @pl.when(c == 0)
    def _():
        s_ref[...] = jnp.zeros_like(s_ref)

    C = RW_CHUNK
    P = 2 * C
    lane = lax.broadcasted_iota(jnp.int32, (C, P), 1)
    lo = lane < R_HEAD_DIM
    ri = lax.broadcasted_iota(jnp.int32, (P, P), 0)
    ci = lax.broadcasted_iota(jnp.int32, (P, P), 1)
    same = (ri // C) == (ci // C)
    strict = jnp.logical_and(same, ri > ci)
    incl = jnp.logical_and(same, ri >= ci)
    eye = jnp.where(ri == ci, 1.0, 0.0).astype(F32)
    mm = _dot_bf16
    zero = jnp.zeros((), at_ref.dtype)

    def stack(x):
        return jnp.concatenate([jnp.where(lo, x, zero), jnp.where(lo, zero, x)], axis=0)

    pairs = range(R_HEADS // 2)
    sls = [slice(p * P, (p + 1) * P) for p in pairs]
    a_s = [stack(at_ref[:, sl]) for sl in sls]
    r_s = [stack(rt_ref[:, sl]) for sl in sls]
    v_s = [stack(v_ref[:, sl]) for sl in sls]
    bk_s = [jnp.concatenate([stack(bt_ref[:, sl]), stack(kt_ref[:, sl])], axis=0) for sl in sls]
    s0 = [s_ref[p] for p in pairs]
    g = [mm(jnp.concatenate([a_s[p], r_s[p]], axis=0), bk_s[p], _NT) for p in pairs]
    l_ab = [jnp.where(strict, g[p][0:P, 0:P], 0.0) for p in pairs]
    l_ak = [jnp.where(strict, g[p][0:P, P:2 * P], 0.0) for p in pairs]
    m_r = [jnp.where(jnp.concatenate([incl, incl], axis=1), g[p][P:2 * P, :], 0.0) for p in pairs]
    t_inv = [eye + l_ab[p] for p in pairs]
    pw = l_ab
    for _ in range(C.bit_length() - 2):
        pw = [mm(pw[p], pw[p]) for p in pairs]
        t_inv = [t_inv[p] + mm(t_inv[p], pw[p]) for p in pairs]
    z = [mm(a_s[p], s0[p], _NT) + mm(l_ak[p], v_s[p]) for p in pairs]
    uv = [jnp.concatenate([mm(t_inv[p], z[p]).astype(at_ref.dtype), v_s[p]], axis=0) for p in pairs]
    y_s = [mm(r_s[p], s0[p], _NT) + mm(m_r[p], uv[p]) for p in pairs]
    s_new = [(s0[p] + mm(uv[p], bk_s[p], _TN)) * gc_ref[0, :, sls[p]] for p in pairs]
    y_ref[...] = jnp.concatenate([y[0:C, :] + y[C:P, :] for y in y_s], axis=-1)
    s_ref[...] = jnp.stack(s_new, axis=0)


def _rwkv_b_call(at, bt, kt, rt, v, gc, B, lp):
    T, W = at.shape
    nc = lp // RW_CHUNK
    rowspec = pl.BlockSpec((RW_CHUNK, W), lambda b, c: (b * nc + c, 0))
    return pl.pallas_call(
        _rwkv_b_kernel,
        grid=(B, nc),
        in_specs=[rowspec, rowspec, rowspec, rowspec, rowspec,
                  pl.BlockSpec((1, 1, W), lambda b, c: (b * nc + c, 0, 0))],
        out_specs=rowspec,
        out_shape=jax.ShapeDtypeStruct((T, W), F32),
        scratch_shapes=[pltpu.VMEM((R_HEADS // 2, 2 * RW_CHUNK, 2 * R_HEAD_DIM), F32)],
        compiler_params=_params(("parallel", "arbitrary")),
        name="rwkv_b",
    )(at, bt, kt, rt, v, gc)


def _merge_kernel(h_ref, oa_ref, y_ref, bonus_ref, g_ref, ug_ref, bg_ref, gng_ref, gnb_ref, hmean_ref,
                  wa_ref, wb_ref, wo_ref, lng_ref, lnb_ref, o_ref, *, alpha):
    y = y_ref[...]
    hmean = hmean_ref[...]
    m = _dot_split(y, hmean)
    yc = y - m
    var = _dot_split(yc * yc, hmean)
    yn = yc * lax.rsqrt(var + GN_EPS) * gng_ref[...] + gnb_ref[...]
    ob = (yn + bonus_ref[...]) * g_ref[...]
    gates = jax.nn.sigmoid(ug_ref[...] + bg_ref[...])
    merged = (gates[:, 0:D_MODEL] * _dot_bf16(oa_ref[...], wa_ref[...])
              + gates[:, D_MODEL:] * _dot_bf16(ob, wb_ref[...]))
    mix = _dot_bf16(merged, wo_ref[...])
    o_ref[...] = _layer_norm(alpha * h_ref[...] + mix, lng_ref[...], lnb_ref[...])


def _merge_call(h, oa, y, bonus, g, ug, b_gate, gn_g, gn_b, wa, wb, wo, ln_g, ln_b, alpha):
    T, D = h.shape
    W = R_WIDTH
    hr = lax.broadcasted_iota(jnp.int32, (W, W), 0) // R_HEAD_DIM
    hc = lax.broadcasted_iota(jnp.int32, (W, W), 1) // R_HEAD_DIM
    hmean = jnp.where(hr == hc, 1.0 / R_HEAD_DIM, 0.0).astype(BF16)
    vec = lambda x: x.reshape(1, -1)
    row = lambda n: pl.BlockSpec((ROW_TILE, n), lambda i: (i, 0))
    const = lambda a, b: pl.BlockSpec((a, b), lambda i: (0, 0))
    return pl.pallas_call(
        functools.partial(_merge_kernel, alpha=alpha),
        grid=(T // ROW_TILE,),
        in_specs=[row(D), row(A_Q), row(W), row(W), row(W), row(GATE_COLS), const(1, GATE_COLS),
                  const(1, W), const(1, W), const(W, W), const(A_Q, D), const(W, D), const(D, D),
                  const(1, D), const(1, D)],
        out_specs=row(D),
        out_shape=jax.ShapeDtypeStruct((T, D), F32),
        compiler_params=_params(("parallel",)),
        name="merge",
    )(h, oa, y, bonus, g, ug, vec(b_gate), vec(gn_g), vec(gn_b), hmean,
      wa.astype(BF16), wb.astype(BF16), wo.astype(BF16), vec(ln_g), vec(ln_b))


MOE_TILES = (1536, 1024, 768, 512, 256)


def _route(logits_t, bias_t):
    s = jax.nn.sigmoid(logits_t)
    sel = s + bias_t
    row = lambda a, e: a[e:e + 1, :]
    gscore = []
    for gi in range(N_GROUPS):
        a, b, c, d = (row(sel, gi * EXPERTS_PER_GROUP + j) for j in range(EXPERTS_PER_GROUP))
        hi1, lo1 = jnp.maximum(a, b), jnp.minimum(a, b)
        hi2, lo2 = jnp.maximum(c, d), jnp.minimum(c, d)
        top1 = jnp.maximum(hi1, hi2)
        top2 = jnp.maximum(jnp.minimum(hi1, hi2), jnp.maximum(lo1, lo2))
        gscore.append(top1 + top2)
    chosen = []
    for gi in range(N_GROUPS):
        ok = None
        for gj in range(N_GROUPS):
            if gj == gi:
                continue
            t = gscore[gi] > gscore[gj] if gj < gi else gscore[gi] >= gscore[gj]
            ok = t if ok is None else jnp.logical_and(ok, t)
        chosen.append(ok)
    picked = []
    for e in range(N_EXPERTS):
        gi = e // EXPERTS_PER_GROUP
        rank = jnp.zeros_like(row(sel, e))
        for e2 in range(gi * EXPERTS_PER_GROUP, (gi + 1) * EXPERTS_PER_GROUP):
            if e2 == e:
                continue
            ahead = row(sel, e2) > row(sel, e) if e2 > e else row(sel, e2) >= row(sel, e)
            rank = rank + jnp.where(ahead, 1.0, 0.0)
        take = jnp.logical_and(chosen[gi], rank < 2.0)
        picked.append(jnp.where(take, row(s, e), 0.0))
    total = picked[0]
    for e in range(1, N_EXPERTS):
        total = total + picked[e]
    return jnp.concatenate(picked, axis=0) / total


def _moe_kernel(h_ref, wr_ref, br_ref, w1_ref, w3_ref, w2_ref, lng_ref, lnb_ref, o_ref,
                xb_ref, comb_ref, acc_ref, *, alpha):
    e = pl.program_id(1)

    @pl.when(e == 0)
    def _():
        x = h_ref[...]
        xb_ref[...] = x.astype(BF16)
        logits_t = _dot(wr_ref[...], x, _NT, precision=lax.Precision.HIGHEST)
        comb_t = _route(logits_t, br_ref[...])
        pad = jnp.zeros((LANES - N_EXPERTS, comb_t.shape[1]), F32)
        comb_ref[...] = jnp.transpose(jnp.concatenate([comb_t, pad], axis=0))
        acc_ref[...] = jnp.zeros_like(acc_ref)

    xb = xb_ref[...]
    h1 = _dot(xb, w1_ref[0])
    h3 = _dot(xb, w3_ref[0])
    act = (h1 * jax.nn.sigmoid(h1)) * h3
    he = _dot(act.astype(BF16), w2_ref[0])
    lane = lax.broadcasted_iota(jnp.int32, comb_ref.shape, 1)
    ce = jnp.sum(jnp.where(lane == e, comb_ref[...], 0.0), axis=-1, keepdims=True)
    acc_ref[...] += ce * he

    @pl.when(e == pl.num_programs(1) - 1)
    def _():
        o_ref[...] = _layer_norm(alpha * h_ref[...] + acc_ref[...], lng_ref[...], lnb_ref[...])


def _moe_call(h, w_router_t, b_router, w1, w3, w2, ln_g, ln_b, alpha, tile):
    T, D = h.shape
    vec = lambda x: x.reshape(1, -1)
    return pl.pallas_call(
        functools.partial(_moe_kernel, alpha=alpha),
        grid=(T // tile, N_EXPERTS),
        in_specs=[pl.BlockSpec((tile, D), lambda i, e: (i, 0)),
                  pl.BlockSpec((N_EXPERTS, D), lambda i, e: (0, 0)),
                  pl.BlockSpec((N_EXPERTS, 1), lambda i, e: (0, 0)),
                  pl.BlockSpec((1, D, D_EXPERT), lambda i, e: (e, 0, 0)),
                  pl.BlockSpec((1, D, D_EXPERT), lambda i, e: (e, 0, 0)),
                  pl.BlockSpec((1, D_EXPERT, D), lambda i, e: (e, 0, 0)),
                  pl.BlockSpec((1, D), lambda i, e: (0, 0)),
                  pl.BlockSpec((1, D), lambda i, e: (0, 0))],
        out_specs=pl.BlockSpec((tile, D), lambda i, e: (i, 0)),
        out_shape=jax.ShapeDtypeStruct((T, D), F32),
        scratch_shapes=[pltpu.VMEM((tile, D), BF16), pltpu.VMEM((tile, LANES), F32),
                        pltpu.VMEM((tile, D), F32)],
        compiler_params=_params(("parallel", "arbitrary")),
        name="moe",
    )(h, w_router_t, b_router.reshape(N_EXPERTS, 1), w1, w3, w2, vec(ln_g), vec(ln_b))


def _split_w_in(w):
    o = 0
    q = w[:, o:o + A_Q]; o += A_Q
    k = w[:, o:o + A_KV]; o += A_KV
    v = w[:, o:o + A_KV]; o += A_KV
    qi = w[:, o:o + IDX_Q]; o += IDX_Q
    kiw = w[:, o:o + IDX_DIM + IDX_HEADS]; o += IDX_DIM + IDX_HEADS
    pad = jnp.zeros((w.shape[0], LANES - IDX_DIM - IDX_HEADS), w.dtype)
    wa = jnp.concatenate([q, qi, k, v, kiw, pad], axis=1)
    wr = w[:, o:o + R_COLS]; o += R_COLS
    wg = w[:, o:o + GATE_COLS]
    return wa.astype(BF16), wr.astype(BF16), wg.astype(BF16)


def _padded_len(L):
    return -(-L // Q_BLOCK) * Q_BLOCK


def kernel(x, meta_tokens, ln_in_g, ln_in_b, w_in, b_gate, rwkv_mu, rwkv_w0, rwkv_w_up, rwkv_a0, rwkv_a_up,
           rwkv_g_up, rwkv_k_k, rwkv_k_a, rwkv_r_k, rwkv_gn_g, rwkv_gn_b, w_branch_a, w_branch_b, w_out,
           ln1_g, ln1_b, ln2_g, ln2_b, w_router, b_router, w_exp1, w_exp3, w_exp2):
    B, S, D = x.shape
    depth = w_in.shape[0]
    alpha = (2 * depth) ** 0.25
    topk = min(TOPK_MAX, S // 4)
    L = N_META + S
    lp = _padded_len(L)
    T = B * lp
    assert T % ROW_TILE == 0
    moe_tile = next(t for t in MOE_TILES if T % t == 0)
    meta = jnp.broadcast_to(meta_tokens[None].astype(x.dtype), (B, N_META, D))
    hin = jnp.concatenate([meta, x, jnp.zeros((B, lp - L, D), x.dtype)], axis=1).reshape(T, D)
    h = _ln_call(hin, ln_in_g, ln_in_b)
    w_router_t = jnp.transpose(w_router)
    for l in range(depth):
        wa, wr, wg = _split_w_in(w_in[l])
        ua, kiw, ur, ug = _proj_call(h, wa, wr, wg)
        oa = _dsa_call(ua, kiw, B, lp, topk)
        at, bt, kt, rt, v, gc, bonus, g = _rwkv_a_call(
            ur, B, lp, rwkv_mu[l], rwkv_w0[l], rwkv_w_up[l], rwkv_a0[l], rwkv_a_up[l], rwkv_g_up[l],
            rwkv_k_k[l], rwkv_k_a[l], rwkv_r_k[l])
        y = _rwkv_b_call(at, bt, kt, rt, v, gc, B, lp)
        h = _merge_call(h, oa, y, bonus, g, ug, b_gate[l], rwkv_gn_g[l], rwkv_gn_b[l],
                        w_branch_a[l], w_branch_b[l], w_out[l], ln1_g[l], ln1_b[l], alpha)
        h = _moe_call(h, w_router_t, b_router, w_exp1[l].astype(BF16), w_exp3[l].astype(BF16),
                      w_exp2[l].astype(BF16), ln2_g[l], ln2_b[l], alpha, moe_tile)
    return h.reshape(B, lp, D)[:, N_META:L]
```

```python
import functools

import jax
import jax.numpy as jnp
from jax import lax
from jax.experimental import pallas as pl
from jax.experimental.pallas import tpu as pltpu

D_MODEL = 1024
CHUNK = 64
N_META = 16
Q_BLOCK = 128
A_HEADS = 8
A_KV_HEADS = 2
A_HEAD_DIM = 64
IDX_HEADS = 8
IDX_DIM = 32
TOPK_MAX = 256
A_Q = A_HEADS * A_HEAD_DIM
A_KV = A_KV_HEADS * A_HEAD_DIM
IDX_Q = IDX_HEADS * IDX_DIM
R_HEADS = 8
R_HEAD_DIM = 64
R_WIDTH = R_HEADS * R_HEAD_DIM
W_LORA = 64
A_LORA = 64
G_LORA = 128
R_COLS = 3 * R_WIDTH + W_LORA + A_LORA + G_LORA
GN_EPS = 64e-5
GATE_COLS = 2 * D_MODEL
N_EXPERTS = 16
N_GROUPS = 4
EXPERTS_PER_GROUP = N_EXPERTS // N_GROUPS
D_EXPERT = 512
LN_EPS = 1e-5

LANES = 128
UA_COLS = A_Q + IDX_Q + 2 * A_KV + LANES
KIW_BLOCK = (A_Q + IDX_Q + 2 * A_KV) // LANES
ROW_TILE = 256
RWKV_A_TILES = (384, 256, 128)
RW_CHUNK = 64
VMEM_LIMIT = 56 * 1024 * 1024

F32 = jnp.float32
BF16 = jnp.bfloat16
INT_MIN = -2147483648


def _dot(a, b, dims=(((1,), (0,)), ((), ())), precision=None):
    return lax.dot_general(a, b, dims, precision=precision, preferred_element_type=F32)


def _dot_bf16(a, b, dims=(((1,), (0,)), ((), ()))):
    return _dot(a.astype(BF16), b.astype(BF16), dims)


_NT = (((1,), (1,)), ((), ()))
_TN = (((0,), (0,)), ((), ()))


def _dot_split(x, m_bf16):
    hi = x.astype(BF16)
    lo = (x - hi.astype(F32)).astype(BF16)
    return _dot(hi, m_bf16) + _dot(lo, m_bf16)


def _layer_norm(x, g, b):
    mu = jnp.mean(x, axis=-1, keepdims=True)
    xc = x - mu
    var = jnp.mean(xc * xc, axis=-1, keepdims=True)
    return xc * lax.rsqrt(var + LN_EPS) * g + b


def _params(sem):
    return pltpu.CompilerParams(dimension_semantics=sem, vmem_limit_bytes=VMEM_LIMIT)


def _ln_kernel(x_ref, g_ref, b_ref, o_ref):
    o_ref[...] = _layer_norm(x_ref[...], g_ref[...], b_ref[...])


def _ln_call(x, g, b):
    T, D = x.shape
    return pl.pallas_call(
        _ln_kernel,
        grid=(T // ROW_TILE,),
        in_specs=[pl.BlockSpec((ROW_TILE, D), lambda i: (i, 0)),
                  pl.BlockSpec((1, D), lambda i: (0, 0)),
                  pl.BlockSpec((1, D), lambda i: (0, 0))],
        out_specs=pl.BlockSpec((ROW_TILE, D), lambda i: (i, 0)),
        out_shape=jax.ShapeDtypeStruct((T, D), F32),
        compiler_params=_params(("parallel",)),
        name="ln_in",
    )(x, g.reshape(1, D), b.reshape(1, D))


def _proj_kernel(h_ref, wa_ref, wr_ref, wg_ref, ua_ref, kiw_ref, ur_ref, ug_ref):
    hb = h_ref[...].astype(BF16)
    ua = _dot(hb, wa_ref[...])
    ua_ref[...] = ua.astype(BF16)
    kiw_ref[...] = ua[:, UA_COLS - LANES:]
    ur_ref[...] = _dot(hb, wr_ref[...])
    ug_ref[...] = _dot(hb, wg_ref[...]).astype(ug_ref.dtype)


def _proj_call(h, wa, wr, wg):
    T, D = h.shape
    row = lambda n: pl.BlockSpec((ROW_TILE, n), lambda i: (i, 0))
    full = lambda n: pl.BlockSpec((D, n), lambda i: (0, 0))
    return pl.pallas_call(
        _proj_kernel,
        grid=(T // ROW_TILE,),
        in_specs=[row(D), full(UA_COLS), full(R_COLS), full(GATE_COLS)],
        out_specs=[row(UA_COLS), row(LANES), row(R_COLS), row(GATE_COLS)],
        out_shape=[jax.ShapeDtypeStruct((T, UA_COLS), BF16),
                   jax.ShapeDtypeStruct((T, LANES), F32),
                   jax.ShapeDtypeStruct((T, R_COLS), F32),
                   jax.ShapeDtypeStruct((T, GATE_COLS), BF16)],
        compiler_params=_params(("parallel",)),
        name="proj",
    )(h, wa, wr, wg)


assert CHUNK & (CHUNK - 1) == 0
CHUNK_SHIFT = CHUNK.bit_length() - 1
BISECT_STEPS = 22
BISECT_ROW_GROUPS = 2
ATT_HEADS_PER_TRIP = 2
REFINE_CAP = 256
KEY_EXTENT_FRACTIONS = (0.18, 0.36, 0.64, 1.0)


def _chunk_id(p):
    return jnp.where(p < N_META, 0, 1 + ((p - N_META) >> CHUNK_SHIFT))


def _count(mask):
    return jnp.sum(jnp.where(mask, 1.0, 0.0), axis=-1, keepdims=True)


def _any(mask):
    return jnp.max(jnp.where(mask, 1.0, 0.0)) > 0.5


NO_INDEX_LIMIT = 2 ** 30


def _bisect_step(sm, kf, st):
    lo, hi, clo, chi = st
    mid = 0.5 * (lo + hi)
    cnt = _count(sm >= mid)
    ge = cnt >= kf
    return (jnp.where(ge, mid, lo), jnp.where(ge, hi, mid), jnp.where(ge, cnt, clo), jnp.where(ge, chi, cnt))


def _select_fixed(sm, n_adm, topk):
    kf = float(topk)
    mx = jnp.max(sm, axis=-1, keepdims=True)
    mn = jnp.min(jnp.where(sm == -jnp.inf, jnp.inf, sm), axis=-1, keepdims=True)
    hi0 = jnp.where(mx >= 0.0, 2.0 * mx + 1.0, 0.5 * mx + 1.0)
    st0 = (mn, hi0, n_adm, jnp.zeros_like(mn))
    rows = sm.shape[0] // BISECT_ROW_GROUPS
    parts = [slice(g * rows, (g + 1) * rows) for g in range(BISECT_ROW_GROUPS)]

    def body(_, sts):
        return tuple(_bisect_step(sm[p], kf, st) for p, st in zip(parts, sts))

    sts = lax.fori_loop(0, BISECT_STEPS, body, tuple(tuple(x[p] for x in st0) for p in parts), unroll=2)
    return tuple(jnp.concatenate([st[c] for st in sts], axis=0) for c in range(4))


def _select_refine(sm, kpos, topk, st):
    kf = float(topk)

    def open_rows(lo, hi, clo):
        band = jnp.logical_and(sm >= lo, sm < hi)
        bmax = jnp.max(jnp.where(band, sm, -jnp.inf), axis=-1, keepdims=True)
        bmin = jnp.min(jnp.where(band, sm, jnp.inf), axis=-1, keepdims=True)
        return _any(jnp.logical_and(clo > kf, bmax != bmin)).astype(jnp.int32)

    def body(c):
        lo, hi, clo, chi = _bisect_step(sm, kf, c[0:4])
        return lo, hi, clo, chi, open_rows(lo, hi, clo), c[5] + 1

    lo, hi, clo, chi = st
    lo, hi, clo, chi, _, _ = lax.while_loop(
        lambda c: jnp.logical_and(c[4] > 0, c[5] < REFINE_CAP), body,
        (lo, hi, clo, chi, open_rows(lo, hi, clo), jnp.int32(0)))
    need = kf - chi
    ri = lax.broadcasted_iota(jnp.int32, (LANES, LANES), 0)
    ci = lax.broadcasted_iota(jnp.int32, (LANES, LANES), 1)
    tri = jnp.where(ri <= ci, 1.0, 0.0).astype(BF16)
    seen = jnp.zeros_like(need)
    j = jnp.full(lo.shape, NO_INDEX_LIMIT, jnp.int32)
    for t in range(sm.shape[1] // LANES):
        cols = slice(t * LANES, (t + 1) * LANES)
        band = jnp.logical_and(sm[:, cols] >= lo, sm[:, cols] < hi)
        cum = _dot(jnp.where(band, 1.0, 0.0).astype(BF16), tri) + seen
        over = jnp.logical_and(band, cum > need)
        j = jnp.minimum(j, jnp.min(jnp.where(over, kpos[:, cols], NO_INDEX_LIMIT), axis=-1, keepdims=True))
        seen = cum[:, LANES - 1:LANES]
    return lo, hi, jnp.where(clo > kf, j, jnp.int32(NO_INDEX_LIMIT))


ST_LO, ST_HI, ST_CLO, ST_CHI = range(4)


def _dsa_scores(wi_ref, qih_ref, kih_ref, sm_ref, st_ref, *, i, topk, nkeys):
    lane = lax.broadcasted_iota(jnp.int32, (Q_BLOCK, LANES), 1)
    sm_ref[:, 0:nkeys] = jnp.zeros((Q_BLOCK, nkeys), F32)

    def idx_head(h, carry):
        s = _dot(qih_ref[h], kih_ref[0:nkeys, :], _NT)
        w = jnp.sum(jnp.where(lane == IDX_DIM + h, wi_ref[...], 0.0), axis=-1, keepdims=True)
        sm_ref[:, 0:nkeys] += w * jnp.maximum(s, 0.0)
        return carry

    lax.fori_loop(0, IDX_HEADS, idx_head, 0)

    qpos = i * Q_BLOCK + lax.broadcasted_iota(jnp.int32, (Q_BLOCK, 1), 0)
    kpos = lax.broadcasted_iota(jnp.int32, (1, nkeys), 1)
    adm = _chunk_id(kpos) <= _chunk_id(qpos)
    sm = jnp.where(adm, sm_ref[:, 0:nkeys], -jnp.inf)
    sm_ref[:, 0:nkeys] = sm
    st = _select_fixed(sm, _count(adm), topk)
    for c, val in enumerate(st):
        st_ref[:, c:c + 1] = val


def _dsa_attend(o_ref, qh_ref, kh_ref, vh_ref, sm_ref, st_ref, j_ref, oh_ref, *, nkeys):
    kpos = lax.broadcasted_iota(jnp.int32, (1, nkeys), 1)
    sm = sm_ref[:, 0:nkeys]
    lo, hi = st_ref[:, ST_LO:ST_LO + 1], st_ref[:, ST_HI:ST_HI + 1]
    sel = jnp.logical_or(sm >= hi, jnp.logical_and(sm >= lo, kpos < j_ref[:, 0:1]))
    sm_ref[:, 0:nkeys] = jnp.where(sel, 0.0, -jnp.inf)
    group = A_HEADS // A_KV_HEADS

    def att_heads(t, carry):
        hs = [t * ATT_HEADS_PER_TRIP + d for d in range(ATT_HEADS_PER_TRIP)]
        ns = [h // group for h in hs]
        s = [_dot(qh_ref[h], kh_ref[n, 0:nkeys, :], _NT) + sm_ref[:, 0:nkeys] for h, n in zip(hs, ns)]
        m = [jnp.max(x, axis=-1, keepdims=True) for x in s]
        p = [jnp.exp(x - mx) for x, mx in zip(s, m)]
        l = [jnp.sum(x, axis=-1, keepdims=True) for x in p]
        o = [_dot(x.astype(BF16), vh_ref[n, 0:nkeys, :]) / lx for x, n, lx in zip(p, ns, l)]
        oh_ref[pl.ds(hs[0], ATT_HEADS_PER_TRIP)] = jnp.stack(o)
        return carry

    lax.fori_loop(0, A_HEADS // ATT_HEADS_PER_TRIP, att_heads, 0)
    o_ref[...] = jnp.concatenate([oh_ref[h] for h in range(A_HEADS)], axis=-1).astype(o_ref.dtype)


def _key_extents(nq):
    return sorted({min(nq, max(1, round(nq * f))) for f in KEY_EXTENT_FRACTIONS} | {nq})


def _dsa_kernel(q_ref, qi_ref, wi_ref, k_ref, v_ref, ki_ref, o_ref,
                qh_ref, qih_ref, kh_ref, vh_ref, kih_ref, sm_ref, st_ref, j_ref, oh_ref, *, topk, nq):
    i = pl.program_id(1)
    lp = nq * Q_BLOCK

    @pl.when(i == 0)
    def _():
        for n in range(A_KV_HEADS):
            kh_ref[n] = k_ref[:, n * A_HEAD_DIM:(n + 1) * A_HEAD_DIM]
            vh_ref[n] = v_ref[:, n * A_HEAD_DIM:(n + 1) * A_HEAD_DIM]
        kih_ref[...] = ki_ref[:, 0:IDX_DIM]

    scale = jnp.asarray(A_HEAD_DIM ** -0.5, BF16)
    for h in range(A_HEADS):
        qh_ref[h] = q_ref[:, h * A_HEAD_DIM:(h + 1) * A_HEAD_DIM] * scale
    for h in range(IDX_HEADS):
        qih_ref[h] = qi_ref[:, h * IDX_DIM:(h + 1) * IDX_DIM]

    last_chunk = 1 + ((i + 1) * Q_BLOCK - 1 - N_META) // CHUNK
    keys_end = N_META + CHUNK * last_chunk
    need = jnp.minimum((keys_end + Q_BLOCK - 1) // Q_BLOCK, nq)
    extents = _key_extents(nq)
    lows = [0] + extents[:-1]
    in_class = [jnp.logical_and(need > a, need <= e) for a, e in zip(lows, extents)]

    for e, on in zip(extents, in_class):
        @pl.when(on)
        def _(e=e):
            _dsa_scores(wi_ref, qih_ref, kih_ref, sm_ref, st_ref, i=i, topk=topk, nkeys=e * Q_BLOCK)

    j_ref[...] = jnp.full(j_ref.shape, NO_INDEX_LIMIT, jnp.int32)

    @pl.when(_any(st_ref[:, ST_CLO:ST_CLO + 1] > float(topk)))
    def _():
        nkeys = Q_BLOCK * sum(jnp.where(on, e, 0) for e, on in zip(extents, in_class))
        kpos = lax.broadcasted_iota(jnp.int32, (1, lp), 1)
        sm = jnp.where(kpos < nkeys, sm_ref[...], -jnp.inf)
        st = tuple(st_ref[:, c:c + 1] for c in range(4))
        lo, hi, j = _select_refine(sm, kpos, topk, st)
        st_ref[:, ST_LO:ST_LO + 1] = lo
        st_ref[:, ST_HI:ST_HI + 1] = hi
        j_ref[:, 0:1] = j

    for e, on in zip(extents, in_class):
        @pl.when(on)
        def _(e=e):
            _dsa_attend(o_ref, qh_ref, kh_ref, vh_ref, sm_ref, st_ref, j_ref, oh_ref, nkeys=e * Q_BLOCK)


def _dsa_call(ua, kiw, B, lp, topk):
    nq = lp // Q_BLOCK
    kern = functools.partial(_dsa_kernel, topk=topk, nq=nq)
    qrow = lambda b, i: b * nq + i
    return pl.pallas_call(
        kern,
        grid=(B, nq),
        in_specs=[pl.BlockSpec((Q_BLOCK, A_Q), lambda b, i: (qrow(b, i), 0)),
                  pl.BlockSpec((Q_BLOCK, IDX_Q), lambda b, i: (qrow(b, i), A_Q // IDX_Q)),
                  pl.BlockSpec((Q_BLOCK, LANES), lambda b, i: (qrow(b, i), 0)),
                  pl.BlockSpec((lp, A_KV), lambda b, i: (b, (A_Q + IDX_Q) // A_KV)),
                  pl.BlockSpec((lp, A_KV), lambda b, i: (b, (A_Q + IDX_Q) // A_KV + 1)),
                  pl.BlockSpec((lp, LANES), lambda b, i: (b, KIW_BLOCK))],
        out_specs=pl.BlockSpec((Q_BLOCK, A_Q), lambda b, i: (qrow(b, i), 0)),
        out_shape=jax.ShapeDtypeStruct((B * lp, A_Q), BF16),
        scratch_shapes=[pltpu.VMEM((A_HEADS, Q_BLOCK, A_HEAD_DIM), BF16),
                        pltpu.VMEM((IDX_HEADS, Q_BLOCK, IDX_DIM), BF16),
                        pltpu.VMEM((A_KV_HEADS, lp, A_HEAD_DIM), BF16),
                        pltpu.VMEM((A_KV_HEADS, lp, A_HEAD_DIM), BF16),
                        pltpu.VMEM((lp, IDX_DIM), BF16),
                        pltpu.VMEM((Q_BLOCK, lp), F32),
                        pltpu.VMEM((Q_BLOCK, LANES), F32),
                        pltpu.VMEM((Q_BLOCK, LANES), jnp.int32),
                        pltpu.VMEM((A_HEADS, Q_BLOCK, A_HEAD_DIM), F32)],
        compiler_params=_params(("parallel", "arbitrary")),
        name="dsa",
    )(ua, ua, kiw, ua, ua, ua)


def _rwkv_a_kernel(ur_ref, mu_ref, w0_ref, wup_ref, a0_ref, aup_ref, gup_ref, kk_ref, ka_ref, rk_ref,
                   tril_ref, hsum_ref,
                   at_ref, bt_ref, kt_ref, rt_ref, v_ref, gc_ref, bonus_ref, g_ref, prev_ref):
    t = pl.program_id(1)
    tm = ur_ref.shape[0]

    @pl.when(t == 0)
    def _():
        prev_ref[...] = jnp.zeros_like(prev_ref)

    u = ur_ref[...]
    rows = lax.broadcasted_iota(jnp.int32, (tm, 1), 0)
    u_prev = jnp.where(rows == 0, prev_ref[0:1, :], pltpu.roll(u, 1, 0))
    prev_ref[0:1, :] = u[tm - 1:tm, :]
    u = u + (u_prev - u) * mu_ref[...]

    W = R_WIDTH
    r, k, v = u[:, 0:W], u[:, W:2 * W], u[:, 2 * W:3 * W]
    xw = u[:, 3 * W:3 * W + W_LORA]
    xa = u[:, 3 * W + W_LORA:3 * W + W_LORA + A_LORA]
    xg = u[:, 3 * W + W_LORA + A_LORA:]

    hp = lax.Precision.HIGHEST
    zw = w0_ref[...] + _dot(jnp.tanh(xw), wup_ref[...], precision=hp)
    logw = -(jnp.maximum(-zw, 0.0) + jnp.log1p(jnp.exp(-jnp.abs(zw)))) - 0.5
    lw = -jnp.exp(logw)
    a = jax.nn.sigmoid(a0_ref[...] + _dot(xa, aup_ref[...], precision=hp))
    g = _dot(jax.nn.sigmoid(xg), gup_ref[...], precision=hp)

    hsum = hsum_ref[...]
    kk = k * kk_ref[...]
    kk = kk * lax.rsqrt(jnp.maximum(_dot_split(kk * kk, hsum), 1e-24))
    kmod = k * (1.0 + (a - 1.0) * ka_ref[...])
    bonus_ref[...] = _dot_split(r * kmod * rk_ref[...], hsum) * v
    g_ref[...] = g
    v_ref[...] = v.astype(v_ref.dtype)

    cs = _dot(tril_ref[...], lw, precision=hp)
    e_neg = jnp.exp(-cs)
    at_ref[...] = (-kk * jnp.exp(cs - lw)).astype(at_ref.dtype)
    bt_ref[...] = (kk * a * e_neg).astype(bt_ref.dtype)
    kt_ref[...] = (kmod * e_neg).astype(kt_ref.dtype)
    rt_ref[...] = (r * jnp.exp(cs)).astype(rt_ref.dtype)
    for c in range(tm // RW_CHUNK):
        gc_ref[c] = jnp.exp(cs[(c + 1) * RW_CHUNK - 1:(c + 1) * RW_CHUNK, :])


def _rwkv_a_call(ur, B, lp, mu, w0, w_up, a0, a_up, g_up, k_k, k_a, r_k):
    T = B * lp
    tm = next(t for t in RWKV_A_TILES if lp % t == 0)
    nt = lp // tm
    W = R_WIDTH
    ri = lax.broadcasted_iota(jnp.int32, (tm, tm), 0)
    ci = lax.broadcasted_iota(jnp.int32, (tm, tm), 1)
    tril = jnp.where(jnp.logical_and(ri >= ci, ri // RW_CHUNK == ci // RW_CHUNK), 1.0, 0.0).astype(F32)
    hr = lax.broadcasted_iota(jnp.int32, (W, W), 0) // R_HEAD_DIM
    hc = lax.broadcasted_iota(jnp.int32, (W, W), 1) // R_HEAD_DIM
    hsum = jnp.where(hr == hc, 1.0, 0.0).astype(BF16)
    vec = lambda x: x.reshape(1, -1)
    const = lambda shape: pl.BlockSpec(shape, lambda b, t: (0,) * len(shape))
    rowspec = lambda n: pl.BlockSpec((tm, n), lambda b, t: (b * nt + t, 0))
    nck = tm // RW_CHUNK
    out_rows = jax.ShapeDtypeStruct((T, W), F32)
    mxu_rows = jax.ShapeDtypeStruct((T, W), BF16)
    return pl.pallas_call(
        _rwkv_a_kernel,
        grid=(B, nt),
        in_specs=[rowspec(R_COLS), const((1, R_COLS)), const((1, W)), const((W_LORA, W)), const((1, W)),
                  const((A_LORA, W)), const((G_LORA, W)), const((1, W)), const((1, W)), const((1, W)),
                  const((tm, tm)), const((W, W))],
        out_specs=[rowspec(W), rowspec(W), rowspec(W), rowspec(W), rowspec(W),
                   pl.BlockSpec((nck, 1, W), lambda b, t: (b * nt + t, 0, 0)),
                   rowspec(W), rowspec(W)],
        out_shape=[mxu_rows, mxu_rows, mxu_rows, mxu_rows, mxu_rows,
                   jax.ShapeDtypeStruct((T // RW_CHUNK, 1, W), F32), out_rows, out_rows],
        scratch_shapes=[pltpu.VMEM((8, R_COLS), F32)],
        compiler_params=_params(("parallel", "arbitrary")),
        name="rwkv_a",
    )(ur, vec(mu), vec(w0), w_up, vec(a0), a_up, g_up, vec(k_k), vec(k_a), vec(r_k), tril, hsum)


def _rwkv_b_kernel(at_ref, bt_ref, kt_ref, rt_ref, v_ref, gc_ref, y_ref, s_ref):
    c = pl.program_id(1)

    @pl.when(c == 0)
    def _():
        s_ref[...] = jnp.zeros_like(s_ref)

    C = RW_CHUNK
    P = 2 * C
    lane = lax.broadcasted_iota(jnp.int32, (C, P), 1)
    lo = lane < R_HEAD_DIM
    ri = lax.broadcasted_iota(jnp.int32, (P, P), 0)
    ci = lax.broadcasted_iota(jnp.int32, (P, P), 1)
    same = (ri // C) == (ci // C)
    strict = jnp.logical_and(same, ri > ci)
    incl = jnp.logical_and(same, ri >= ci)
    eye = jnp.where(ri == ci, 1.0, 0.0).astype(F32)
    mm = _dot_bf16
    zero = jnp.zeros((), at_ref.dtype)

    def stack(x):
        return jnp.concatenate([jnp.where(lo, x, zero), jnp.where(lo, zero, x)], axis=0)

    pairs = range(R_HEADS // 2)
    sls = [slice(p * P, (p + 1) * P) for p in pairs]
    a_s = [stack(at_ref[:, sl]) for sl in sls]
    r_s = [stack(rt_ref[:, sl]) for sl in sls]
    v_s = [stack(v_ref[:, sl]) for sl in sls]
    bk_s = [jnp.concatenate([stack(bt_ref[:, sl]), stack(kt_ref[:, sl])], axis=0) for sl in sls]
    s0 = [s_ref[p] for p in pairs]
    g = [mm(jnp.concatenate([a_s[p], r_s[p]], axis=0), bk_s[p], _NT) for p in pairs]
    l_ab = [jnp.where(strict, g[p][0:P, 0:P], 0.0) for p in pairs]
    l_ak = [jnp.where(strict, g[p][0:P, P:2 * P], 0.0) for p in pairs]
    m_r = [jnp.where(jnp.concatenate([incl, incl], axis=1), g[p][P:2 * P, :], 0.0) for p in pairs]
    t_inv = [eye + l_ab[p] for p in pairs]
    pw = l_ab
    for _ in range(C.bit_length() - 2):
        pw = [mm(pw[p], pw[p]) for p in pairs]
        t_inv = [t_inv[p] + mm(t_inv[p], pw[p]) for p in pairs]
    z = [mm(a_s[p], s0[p], _NT) + mm(l_ak[p], v_s[p]) for p in pairs]
    uv = [jnp.concatenate([mm(t_inv[p], z[p]).astype(at_ref.dtype), v_s[p]], axis=0) for p in pairs]
    y_s = [mm(r_s[p], s0[p], _NT) + mm(m_r[p], uv[p]) for p in pairs]
    s_new = [(s0[p] + mm(uv[p], bk_s[p], _TN)) * gc_ref[0, :, sls[p]] for p in pairs]
    y_ref[...] = jnp.concatenate([y[0:C, :] + y[C:P, :] for y in y_s], axis=-1)
    s_ref[...] = jnp.stack(s_new, axis=0)


def _rwkv_b_call(at, bt, kt, rt, v, gc, B, lp):
    T, W = at.shape
    nc = lp // RW_CHUNK
    rowspec = pl.BlockSpec((RW_CHUNK, W), lambda b, c: (b * nc + c, 0))
    return pl.pallas_call(
        _rwkv_b_kernel,
        grid=(B, nc),
        in_specs=[rowspec, rowspec, rowspec, rowspec, rowspec,
                  pl.BlockSpec((1, 1, W), lambda b, c: (b * nc + c, 0, 0))],
        out_specs=rowspec,
        out_shape=jax.ShapeDtypeStruct((T, W), F32),
        scratch_shapes=[pltpu.VMEM((R_HEADS // 2, 2 * RW_CHUNK, 2 * R_HEAD_DIM), F32)],
        compiler_params=_params(("parallel", "arbitrary")),
        name="rwkv_b",
    )(at, bt, kt, rt, v, gc)


def _merge_kernel(h_ref, oa_ref, y_ref, bonus_ref, g_ref, ug_ref, bg_ref, gng_ref, gnb_ref, hmean_ref,
                  wa_ref, wb_ref, wo_ref, lng_ref, lnb_ref, o_ref, *, alpha):
    y = y_ref[...]
    hmean = hmean_ref[...]
    m = _dot_split(y, hmean)
    yc = y - m
    var = _dot_split(yc * yc, hmean)
    yn = yc * lax.rsqrt(var + GN_EPS) * gng_ref[...] + gnb_ref[...]
    ob = (yn + bonus_ref[...]) * g_ref[...]
    gates = jax.nn.sigmoid(ug_ref[...].astype(F32) + bg_ref[...])
    merged = (gates[:, 0:D_MODEL] * _dot_bf16(oa_ref[...], wa_ref[...])
              + gates[:, D_MODEL:] * _dot_bf16(ob, wb_ref[...]))
    mix = _dot_bf16(merged, wo_ref[...])
    o_ref[...] = _layer_norm(alpha * h_ref[...] + mix, lng_ref[...], lnb_ref[...])


def _merge_call(h, oa, y, bonus, g, ug, b_gate, gn_g, gn_b, wa, wb, wo, ln_g, ln_b, alpha):
    T, D = h.shape
    W = R_WIDTH
    hr = lax.broadcasted_iota(jnp.int32, (W, W), 0) // R_HEAD_DIM
    hc = lax.broadcasted_iota(jnp.int32, (W, W), 1) // R_HEAD_DIM
    hmean = jnp.where(hr == hc, 1.0 / R_HEAD_DIM, 0.0).astype(BF16)
    vec = lambda x: x.reshape(1, -1)
    row = lambda n: pl.BlockSpec((ROW_TILE, n), lambda i: (i, 0))
    const = lambda a, b: pl.BlockSpec((a, b), lambda i: (0, 0))
    return pl.pallas_call(
        functools.partial(_merge_kernel, alpha=alpha),
        grid=(T // ROW_TILE,),
        in_specs=[row(D), row(A_Q), row(W), row(W), row(W), row(GATE_COLS), const(1, GATE_COLS),
                  const(1, W), const(1, W), const(W, W), const(A_Q, D), const(W, D), const(D, D),
                  const(1, D), const(1, D)],
        out_specs=row(D),
        out_shape=jax.ShapeDtypeStruct((T, D), F32),
        compiler_params=_params(("parallel",)),
        name="merge",
    )(h, oa, y, bonus, g, ug, vec(b_gate), vec(gn_g), vec(gn_b), hmean,
      wa.astype(BF16), wb.astype(BF16), wo.astype(BF16), vec(ln_g), vec(ln_b))


MOE_TILES = (1536, 1024, 768, 512, 256)


def _route(logits_t, bias_t):
    s = jax.nn.sigmoid(logits_t)
    sel = s + bias_t
    row = lambda a, e: a[e:e + 1, :]
    gscore = []
    for gi in range(N_GROUPS):
        a, b, c, d = (row(sel, gi * EXPERTS_PER_GROUP + j) for j in range(EXPERTS_PER_GROUP))
        hi1, lo1 = jnp.maximum(a, b), jnp.minimum(a, b)
        hi2, lo2 = jnp.maximum(c, d), jnp.minimum(c, d)
        top1 = jnp.maximum(hi1, hi2)
        top2 = jnp.maximum(jnp.minimum(hi1, hi2), jnp.maximum(lo1, lo2))
        gscore.append(top1 + top2)
    chosen = []
    for gi in range(N_GROUPS):
        ok = None
        for gj in range(N_GROUPS):
            if gj == gi:
                continue
            t = gscore[gi] > gscore[gj] if gj < gi else gscore[gi] >= gscore[gj]
            ok = t if ok is None else jnp.logical_and(ok, t)
        chosen.append(ok)
    picked = []
    for e in range(N_EXPERTS):
        gi = e // EXPERTS_PER_GROUP
        rank = jnp.zeros_like(row(sel, e))
        for e2 in range(gi * EXPERTS_PER_GROUP, (gi + 1) * EXPERTS_PER_GROUP):
            if e2 == e:
                continue
            ahead = row(sel, e2) > row(sel, e) if e2 > e else row(sel, e2) >= row(sel, e)
            rank = rank + jnp.where(ahead, 1.0, 0.0)
        take = jnp.logical_and(chosen[gi], rank < 2.0)
        picked.append(jnp.where(take, row(s, e), 0.0))
    total = picked[0]
    for e in range(1, N_EXPERTS):
        total = total + picked[e]
    return jnp.concatenate(picked, axis=0) / total


def _moe_kernel(h_ref, wr_ref, br_ref, w1_ref, w3_ref, w2_ref, lng_ref, lnb_ref, o_ref,
                xb_ref, comb_ref, acc_ref, *, alpha):
    e = pl.program_id(1)

    @pl.when(e == 0)
    def _():
        x = h_ref[...]
        xb_ref[...] = x.astype(BF16)
        logits_t = _dot(wr_ref[...], x, _NT, precision=lax.Precision.HIGHEST)
        comb_t = _route(logits_t, br_ref[...])
        pad = jnp.zeros((LANES - N_EXPERTS, comb_t.shape[1]), F32)
        comb_ref[...] = jnp.transpose(jnp.concatenate([comb_t, pad], axis=0))
        acc_ref[...] = jnp.zeros_like(acc_ref)

    xb = xb_ref[...]
    h1 = _dot(xb, w1_ref[0])
    h3 = _dot(xb, w3_ref[0])
    act = (h1 * jax.nn.sigmoid(h1)) * h3
    he = _dot(act.astype(BF16), w2_ref[0])
    lane = lax.broadcasted_iota(jnp.int32, comb_ref.shape, 1)
    ce = jnp.sum(jnp.where(lane == e, comb_ref[...], 0.0), axis=-1, keepdims=True)
    acc_ref[...] += ce * he

    @pl.when(e == pl.num_programs(1) - 1)
    def _():
        o_ref[...] = _layer_norm(alpha * h_ref[...] + acc_ref[...], lng_ref[...], lnb_ref[...])


def _moe_call(h, w_router_t, b_router, w1, w3, w2, ln_g, ln_b, alpha, tile):
    T, D = h.shape
    vec = lambda x: x.reshape(1, -1)
    return pl.pallas_call(
        functools.partial(_moe_kernel, alpha=alpha),
        grid=(T // tile, N_EXPERTS),
        in_specs=[pl.BlockSpec((tile, D), lambda i, e: (i, 0)),
                  pl.BlockSpec((N_EXPERTS, D), lambda i, e: (0, 0)),
                  pl.BlockSpec((N_EXPERTS, 1), lambda i, e: (0, 0)),
                  pl.BlockSpec((1, D, D_EXPERT), lambda i, e: (e, 0, 0)),
                  pl.BlockSpec((1, D, D_EXPERT), lambda i, e: (e, 0, 0)),
                  pl.BlockSpec((1, D_EXPERT, D), lambda i, e: (e, 0, 0)),
                  pl.BlockSpec((1, D), lambda i, e: (0, 0)),
                  pl.BlockSpec((1, D), lambda i, e: (0, 0))],
        out_specs=pl.BlockSpec((tile, D), lambda i, e: (i, 0)),
        out_shape=jax.ShapeDtypeStruct((T, D), F32),
        scratch_shapes=[pltpu.VMEM((tile, D), BF16), pltpu.VMEM((tile, LANES), F32),
                        pltpu.VMEM((tile, D), F32)],
        compiler_params=_params(("parallel", "arbitrary")),
        name="moe",
    )(h, w_router_t, b_router.reshape(N_EXPERTS, 1), w1, w3, w2, vec(ln_g), vec(ln_b))


def _split_w_in(w):
    o = 0
    q = w[:, o:o + A_Q]; o += A_Q
    k = w[:, o:o + A_KV]; o += A_KV
    v = w[:, o:o + A_KV]; o += A_KV
    qi = w[:, o:o + IDX_Q]; o += IDX_Q
    kiw = w[:, o:o + IDX_DIM + IDX_HEADS]; o += IDX_DIM + IDX_HEADS
    pad = jnp.zeros((w.shape[0], LANES - IDX_DIM - IDX_HEADS), w.dtype)
    wa = jnp.concatenate([q, qi, k, v, kiw, pad], axis=1)
    wr = w[:, o:o + R_COLS]; o += R_COLS
    wg = w[:, o:o + GATE_COLS]
    return wa.astype(BF16), wr.astype(BF16), wg.astype(BF16)


def _padded_len(L):
    return -(-L // Q_BLOCK) * Q_BLOCK


def kernel(x, meta_tokens, ln_in_g, ln_in_b, w_in, b_gate, rwkv_mu, rwkv_w0, rwkv_w_up, rwkv_a0, rwkv_a_up,
           rwkv_g_up, rwkv_k_k, rwkv_k_a, rwkv_r_k, rwkv_gn_g, rwkv_gn_b, w_branch_a, w_branch_b, w_out,
           ln1_g, ln1_b, ln2_g, ln2_b, w_router, b_router, w_exp1, w_exp3, w_exp2):
    B, S, D = x.shape
    depth = w_in.shape[0]
    alpha = (2 * depth) ** 0.25
    topk = min(TOPK_MAX, S // 4)
    L = N_META + S
    lp = _padded_len(L)
    T = B * lp
    assert T % ROW_TILE == 0
    moe_tile = next(t for t in MOE_TILES if T % t == 0)
    meta = jnp.broadcast_to(meta_tokens[None].astype(x.dtype), (B, N_META, D))
    hin = jnp.concatenate([meta, x, jnp.zeros((B, lp - L, D), x.dtype)], axis=1).reshape(T, D)
    h = _ln_call(hin, ln_in_g, ln_in_b)
    w_router_t = jnp.transpose(w_router)
    for l in range(depth):
        wa, wr, wg = _split_w_in(w_in[l])
        ua, kiw, ur, ug = _proj_call(h, wa, wr, wg)
        oa = _dsa_call(ua, kiw, B, lp, topk)
        at, bt, kt, rt, v, gc, bonus, g = _rwkv_a_call(
            ur, B, lp, rwkv_mu[l], rwkv_w0[l], rwkv_w_up[l], rwkv_a0[l], rwkv_a_up[l], rwkv_g_up[l],
            rwkv_k_k[l], rwkv_k_a[l], rwkv_r_k[l])
        y = _rwkv_b_call(at, bt, kt, rt, v, gc, B, lp)
        h = _merge_call(h, oa, y, bonus, g, ug, b_gate[l], rwkv_gn_g[l], rwkv_gn_b[l],
                        w_branch_a[l], w_branch_b[l], w_out[l], ln1_g[l], ln1_b[l], alpha)
        h = _moe_call(h, w_router_t, b_router, w_exp1[l].astype(BF16), w_exp3[l].astype(BF16),
                      w_exp2[l].astype(BF16), ln2_g[l], ln2_b[l], alpha, moe_tile)
    return h.reshape(B, lp, D)[:, N_META:L]
```
